```python
import math
import jax, jax.numpy as jnp
from jax import lax
import numpy as np

D_MODEL = 4096
BATCH = 1
SEQ = 16384
DEPTH = 2

F32 = jnp.float32
EPS = 1e-6
N_MOD = 6
D_MIX = D_MODEL

A_HEADS = 8
A_HEAD_DIM = 128
A_WIDTH = A_HEADS * A_HEAD_DIM
CONV_W = 4
LRU_C = 8.0

B_HEADS = 12
B_HEAD_DIM = 128
B_WIDTH = B_HEADS * B_HEAD_DIM
DN_CHUNK = 64

C_HEADS = 12
C_NOPE = 128
C_ROPE = 64
C_VDIM = 128
C_WIDTH = C_HEADS * C_VDIM
Q_LORA = 1024
KV_LORA = 512
ROPE_THETA = 10000.0
ATTN_BLOCK = 128

PEER_HEADS = 8
PEER_KEYS = 128
PEER_N = PEER_KEYS * PEER_KEYS
PEER_QDIM = 256
PEER_HALF = PEER_QDIM // 2
PEER_TOPK = 16
PEER_CHUNK = 64

IN_SPLITS = (A_WIDTH, A_WIDTH,
             B_WIDTH, B_WIDTH, B_WIDTH, B_WIDTH,
             B_HEADS, B_HEADS,
             Q_LORA, KV_LORA, C_ROPE)
D_IN = 2 * A_WIDTH + 4 * B_WIDTH + 2 * B_HEADS + Q_LORA + KV_LORA + C_ROPE

kernel_name = "hybrid_rglru_gdn_mla_peer_adaln"


def rmsnorm(x, w):
    xf = x.astype(F32)
    y = xf * lax.rsqrt(jnp.mean(xf * xf, axis=-1, keepdims=True) + EPS)
    return (y * w.astype(F32)).astype(x.dtype)


def l2norm(x):
    return x * lax.rsqrt(jnp.sum(x * x, axis=-1, keepdims=True) + EPS)


def split_columns(p):
    outs, start = [], 0
    for width in IN_SPLITS:
        outs.append(p[..., start:start + width])
        start += width
    return outs


def causal_dwconv(x, w, b=None):
    ch = x.shape[-1]
    y = lax.conv_general_dilated(x, w[:, None, :].astype(x.dtype), window_strides=(1,),
                                 padding=[(w.shape[0] - 1, 0)],
                                 dimension_numbers=('NWC', 'WIO', 'NWC'),
                                 feature_group_count=ch)
    return y if b is None else y + b


def rglru_group(x_rec, x_gate, conv_w, conv_b, w_a, b_a, w_x, b_x, lam):
    bsz, s, _ = x_rec.shape
    u = causal_dwconv(x_rec, conv_w, conv_b)
    uh = u.reshape(bsz, s, A_HEADS, A_HEAD_DIM)
    r = jax.nn.sigmoid(jnp.einsum('bshi,hij->bshj', uh, w_a).reshape(bsz, s, A_WIDTH) + b_a)
    i = jax.nn.sigmoid(jnp.einsum('bshi,hij->bshj', uh, w_x).reshape(bsz, s, A_WIDTH) + b_x)
    log_a = -LRU_C * r.astype(F32) * jax.nn.softplus(-lam.astype(F32))
    a = jnp.exp(log_a)
    bterm = jnp.sqrt(jnp.maximum(-jnp.expm1(2.0 * log_a), 0.0)) * (i * u).astype(F32)

    def combine(lhs, rhs):
        a1, b1 = lhs
        a2, b2 = rhs
        return a1 * a2, a2 * b1 + b2

    _, h = lax.associative_scan(combine, (a, bterm), axis=1)
    return jax.nn.gelu(x_gate) * h.astype(x_rec.dtype)


def chunk_gated_delta_rule(q, k, v, g, beta):
    bsz, s, h, d = q.shape
    c = DN_CHUNK
    n = s // c

    def chunks(t):
        return t.reshape(bsz, n, c, h, -1).transpose(0, 1, 3, 2, 4)

    q, k, v = chunks(q), chunks(k), chunks(v)
    g = g.reshape(bsz, n, c, h).transpose(0, 1, 3, 2)
    beta = beta.reshape(bsz, n, c, h).transpose(0, 1, 3, 2)
    gc = jnp.cumsum(g, axis=-1)
    lower = jnp.tril(jnp.ones((c, c), bool))
    lower_strict = jnp.tril(jnp.ones((c, c), bool), -1)
    diff = gc[..., :, None] - gc[..., None, :]
    decay = jnp.where(lower, jnp.exp(jnp.where(lower, diff, 0.0)), 0.0)
    kb = k * beta[..., None]
    lmat = jnp.where(lower_strict, jnp.einsum('bnhid,bnhjd->bnhij', kb, k) * decay, 0.0)
    amat = lmat + jnp.eye(c, dtype=F32)
    rhs = jnp.concatenate([v * beta[..., None], kb * jnp.exp(gc)[..., None]], axis=-1)
    sol = lax.linalg.triangular_solve(amat, rhs, left_side=True, lower=True, unit_diagonal=True)
    u, w = sol[..., :d], sol[..., d:]
    qk = jnp.where(lower, jnp.einsum('bnhid,bnhjd->bnhij', q, k) * decay, 0.0)
    q_dec = q * jnp.exp(gc)[..., None]
    k_tail = k * jnp.exp(gc[..., -1:] - gc)[..., None]
    g_last = jnp.exp(gc[..., -1])

    def step(state, inp):
        q_d, qk_i, u_i, w_i, k_t, gl = inp
        v_new = u_i - jnp.einsum('bhcd,bhde->bhce', w_i, state)
        o = jnp.einsum('bhcd,bhde->bhce', q_d, state) + jnp.einsum('bhij,bhje->bhie', qk_i, v_new)
        state = state * gl[..., None, None] + jnp.einsum('bhcd,bhce->bhde', k_t, v_new)
        return state, o

    xs = tuple(t.swapaxes(0, 1) for t in (q_dec, qk, u, w, k_tail, g_last))
    s0 = jnp.zeros((bsz, h, d, d), F32)
    _, o = lax.scan(step, s0, xs)
    return o.transpose(1, 0, 3, 2, 4).reshape(bsz, s, h, d)


def gated_deltanet_group(q, k, v, z, beta_raw, alpha_raw, conv_w, a_log, dt_bias, norm_w):
    bsz, s, _ = q.shape
    qkv = jax.nn.silu(causal_dwconv(jnp.concatenate([q, k, v], axis=-1), conv_w))
    q, k, v = jnp.split(qkv, 3, axis=-1)

    def heads(t):
        return t.reshape(bsz, s, B_HEADS, B_HEAD_DIM).astype(F32)

    qh = l2norm(heads(q)) * (B_HEAD_DIM ** -0.5)
    kh = l2norm(heads(k))
    vh = heads(v)
    beta = jax.nn.sigmoid(beta_raw.astype(F32))
    g = -jnp.exp(a_log.astype(F32)) * jax.nn.softplus(alpha_raw.astype(F32) + dt_bias.astype(F32))
    o = chunk_gated_delta_rule(qh, kh, vh, g, beta)
    o = rmsnorm(o, norm_w) * jax.nn.silu(heads(z))
    return o.reshape(bsz, s, B_WIDTH).astype(q.dtype)


def rope_cos_sin(positions):
    inv = ROPE_THETA ** (-jnp.arange(0, C_ROPE, 2, dtype=F32) / C_ROPE)
    ang = positions.astype(F32)[..., None] * inv
    return jnp.cos(ang), jnp.sin(ang)


def apply_rope(x, cos, sin):
    half = x.shape[-1] // 2
    xf = x.astype(F32)
    x1, x2 = xf[..., :half], xf[..., half:]
    return jnp.concatenate([x1 * cos - x2 * sin, x1 * sin + x2 * cos], axis=-1).astype(x.dtype)


def causal_block_attention(q, k, v, scale):
    bsz, s, h, dq = q.shape
    nb = s // ATTN_BLOCK
    qb = q.reshape(bsz, nb, ATTN_BLOCK, h, dq).transpose(1, 0, 2, 3, 4)
    kpos = jnp.arange(s)
    neg = jnp.finfo(F32).min

    def one(args):
        q_blk, bi = args
        sc = jnp.einsum('bqhd,bkhd->bhqk', q_blk, k).astype(F32) * scale
        qpos = bi * ATTN_BLOCK + jnp.arange(ATTN_BLOCK)
        sc = jnp.where(kpos[None, :] <= qpos[:, None], sc, neg)
        p = jax.nn.softmax(sc, axis=-1).astype(v.dtype)
        return jnp.einsum('bhqk,bkhd->bqhd', p, v)

    o = lax.map(one, (qb, jnp.arange(nb)))
    return o.transpose(1, 0, 2, 3, 4).reshape(bsz, s, h, -1)


def mla_group(c_q, c_kv, k_rope, positions, q_norm_w, w_q_up, kv_norm_w, w_kv_up):
    bsz, s, _ = c_q.shape
    q = jnp.einsum('bsr,rhd->bshd', rmsnorm(c_q, q_norm_w), w_q_up)
    kv = jnp.einsum('bsr,rhd->bshd', rmsnorm(c_kv, kv_norm_w), w_kv_up)
    q_nope, q_pe = q[..., :C_NOPE], q[..., C_NOPE:]
    k_nope, v = kv[..., :C_NOPE], kv[..., C_NOPE:]
    cos, sin = rope_cos_sin(positions)
    q_pe = apply_rope(q_pe, cos[:, :, None, :], sin[:, :, None, :])
    k_pe = apply_rope(k_rope, cos, sin)
    q_full = jnp.concatenate([q_nope, q_pe], axis=-1)
    k_full = jnp.concatenate([k_nope, jnp.broadcast_to(k_pe[:, :, None, :], (bsz, s, C_HEADS, C_ROPE))], axis=-1)
    o = causal_block_attention(q_full, k_full, v, (C_NOPE + C_ROPE) ** -0.5)
    return o.reshape(bsz, s, C_WIDTH)


def peer_ffn(h, w_query, sub_keys, expert_u, expert_v):
    bsz, s, d = h.shape
    ht = h.reshape(bsz * s // PEER_CHUNK, PEER_CHUNK, d)

    def one(xc):
        q = jnp.einsum('td,dhq->thq', xc, w_query).reshape(PEER_CHUNK, PEER_HEADS, 2, PEER_HALF)
        sc = jnp.einsum('thpq,hpnq->thpn', q, sub_keys).astype(F32)
        s_top, i_top = lax.top_k(sc, PEER_TOPK)
        cand = s_top[:, :, 0, :, None] + s_top[:, :, 1, None, :]
        cand_idx = i_top[:, :, 0, :, None] * PEER_KEYS + i_top[:, :, 1, None, :]
        best, pos = lax.top_k(cand.reshape(PEER_CHUNK, PEER_HEADS, -1), PEER_TOPK)
        eidx = jnp.take_along_axis(cand_idx.reshape(PEER_CHUNK, PEER_HEADS, -1), pos, axis=-1)
        gate = jax.nn.softmax(best, axis=-1)
        u = expert_u[eidx]
        act = jax.nn.gelu(jnp.einsum('td,thkd->thk', xc, u).astype(F32))
        coef = (gate * act).astype(xc.dtype)
        return jnp.einsum('thk,thkd->td', coef, expert_v[eidx])

    return lax.map(one, ht).reshape(bsz, s, d)


def setup_inputs(seed: int = 0) -> dict:
    key = jax.random.key(seed)
    ks = iter(jax.random.split(key, 40))
    nrm = lambda shape, scale: jax.random.normal(next(ks), shape, F32) * scale
    gain = lambda shape: 1.0 + 0.05 * jax.random.normal(next(ks), shape, F32)
    L, D = DEPTH, D_MODEL
    x = jax.random.normal(next(ks), (BATCH, SEQ, D), F32)
    c = jax.random.normal(next(ks), (BATCH, D), F32)
    offset = jax.random.randint(next(ks), (BATCH, 1), 0, 4096, dtype=jnp.int32)
    positions = offset + jnp.arange(SEQ, dtype=jnp.int32)[None, :]
    a0 = jax.random.uniform(next(ks), (L, A_WIDTH), F32, 0.9, 0.999) ** (1.0 / LRU_C)
    lru_lambda = jnp.log(a0) - jnp.log1p(-a0)
    dn_a_log = jnp.log(jax.random.uniform(next(ks), (L, B_HEADS), F32, 1.0, 16.0))
    dt = jnp.exp(jax.random.uniform(next(ks), (L, B_HEADS), F32, math.log(1e-3), math.log(1e-1)))
    dn_dt_bias = dt + jnp.log(-jnp.expm1(-dt))
    return {
        "x": x,
        "c": c,
        "positions": positions,
        "mod_w": nrm((D, N_MOD * D), 0.5 * D ** -0.5),
        "mod_layer": nrm((L, N_MOD, D), 0.3),
        "norm_mix_w": gain((L, D)),
        "w_in": nrm((L, D, D_IN), D ** -0.5),
        "lru_conv_w": nrm((L, CONV_W, A_WIDTH), 0.5),
        "lru_conv_b": nrm((L, A_WIDTH), 0.02),
        "lru_wa": nrm((L, A_HEADS, A_HEAD_DIM, A_HEAD_DIM), A_HEAD_DIM ** -0.5),
        "lru_ba": nrm((L, A_WIDTH), 0.1),
        "lru_wx": nrm((L, A_HEADS, A_HEAD_DIM, A_HEAD_DIM), A_HEAD_DIM ** -0.5),
        "lru_bx": nrm((L, A_WIDTH), 0.1),
        "lru_lambda": lru_lambda,
        "dn_conv_w": nrm((L, CONV_W, 3 * B_WIDTH), 0.5),
        "dn_a_log": dn_a_log,
        "dn_dt_bias": dn_dt_bias,
        "dn_norm_w": gain((L, B_HEAD_DIM)),
        "mla_q_norm_w": gain((L, Q_LORA)),
        "mla_w_q_up": nrm((L, Q_LORA, C_HEADS, C_NOPE + C_ROPE), Q_LORA ** -0.5),
        "mla_kv_norm_w": gain((L, KV_LORA)),
        "mla_w_kv_up": nrm((L, KV_LORA, C_HEADS, C_NOPE + C_VDIM), KV_LORA ** -0.5),
        "branch_norm_a": gain((L, A_WIDTH)),
        "branch_norm_c": gain((L, C_WIDTH)),
        "w_out": nrm((L, D_MIX, D), D_MIX ** -0.5),
        "norm_ffn_w": gain((L, D)),
        "peer_w_query": nrm((L, D, PEER_HEADS, PEER_QDIM), D ** -0.5),
        "peer_sub_keys": nrm((L, PEER_HEADS, 2, PEER_KEYS, PEER_HALF), PEER_HALF ** -0.5),
        "peer_u": nrm((L, PEER_N, D), D ** -0.5),
        "peer_v": nrm((L, PEER_N, D), 0.5),
        "final_norm_w": gain((D,)),
    }


def reference(x, c, positions, mod_w, mod_layer, norm_mix_w, w_in, lru_conv_w, lru_conv_b,
              lru_wa, lru_ba, lru_wx, lru_bx, lru_lambda, dn_conv_w, dn_a_log, dn_dt_bias,
              dn_norm_w, mla_q_norm_w, mla_w_q_up, mla_kv_norm_w, mla_w_kv_up, branch_norm_a,
              branch_norm_c, w_out, norm_ffn_w, peer_w_query, peer_sub_keys, peer_u, peer_v,
              final_norm_w):
    bsz = x.shape[0]
    base = (jax.nn.silu(c) @ mod_w).reshape(bsz, N_MOD, D_MODEL)
    for l in range(DEPTH):
        m = base + mod_layer[l]
        sh_a, sc_a, g_a, sh_f, sc_f, g_f = [m[:, i, None, :] for i in range(N_MOD)]

        h = rmsnorm(x, norm_mix_w[l]) * (1.0 + sc_a) + sh_a
        (a_rec, a_gate, b_q, b_k, b_v, b_z, b_beta, b_alpha,
         c_q, c_kv, c_kr) = split_columns(h @ w_in[l])
        ya = rglru_group(a_rec, a_gate, lru_conv_w[l], lru_conv_b[l], lru_wa[l], lru_ba[l],
                         lru_wx[l], lru_bx[l], lru_lambda[l])
        yb = gated_deltanet_group(b_q, b_k, b_v, b_z, b_beta, b_alpha, dn_conv_w[l],
                                  dn_a_log[l], dn_dt_bias[l], dn_norm_w[l])
        yc = mla_group(c_q, c_kv, c_kr, positions, mla_q_norm_w[l], mla_w_q_up[l],
                       mla_kv_norm_w[l], mla_w_kv_up[l])
        y = jnp.concatenate([rmsnorm(ya, branch_norm_a[l]), yb, rmsnorm(yc, branch_norm_c[l])], axis=-1)
        x = x + g_a * (y @ w_out[l])

        h = rmsnorm(x, norm_ffn_w[l]) * (1.0 + sc_f) + sh_f
        x = x + g_f * peer_ffn(h, peer_w_query[l], peer_sub_keys[l], peer_u[l], peer_v[l])
    return rmsnorm(x, final_norm_w)
```

```python
import functools
import math

import jax
import jax.numpy as jnp
from jax import lax
from jax.experimental import pallas as pl
from jax.experimental.pallas import tpu as pltpu

F32 = jnp.float32
BF16 = jnp.bfloat16
EPS = 1e-6
N_MOD = 6
LRU_C = 8.0
CONV_W = 4
C_NOPE = 128
C_ROPE = 64
ROPE_THETA = 10000.0
PEER_TOPK = 16
DN_CHUNK = 128
DN_BASE = 16
LANES = 128
NEG_BIG = -1e30

_VMEM_LIMIT = 56 * 1024 * 1024


def _cparams(sem, vmem=_VMEM_LIMIT):
    return pltpu.CompilerParams(dimension_semantics=sem, vmem_limit_bytes=vmem)


def _pick(n, candidates):
    for c in candidates:
        if n % c == 0:
            return c
    raise ValueError(f"no tile in {candidates} divides {n}")


def _sigmoid(x):
    return 1.0 / (1.0 + jnp.exp(-x))


def _softplus(x):
    return jnp.maximum(x, 0.0) + jnp.log1p(jnp.exp(-jnp.abs(x)))


def _silu(x):
    return x * _sigmoid(x)


def _gelu_tanh(x):
    c = math.sqrt(2.0 / math.pi)
    return 0.5 * x * (1.0 + jnp.tanh(c * (x + 0.044715 * (x * x * x))))


def _rms(x, w):
    ms = jnp.mean(x * x, axis=-1, keepdims=True)
    return x * lax.rsqrt(ms + EPS) * w


def _dot(a, b):
    return jnp.dot(a, b, preferred_element_type=F32)


def _dot_nt(a, b):
    return lax.dot_general(a, b, (((1,), (1,)), ((), ())), preferred_element_type=F32)


def _mod_kernel(c_ref, w_ref, ml_ref, o_ref, acc_ref):
    k = pl.program_id(1)

    @pl.when(k == 0)
    def _():
        acc_ref[...] = jnp.zeros_like(acc_ref)

    c = c_ref[...]
    sc = jnp.broadcast_to(_silu(c), (8, c.shape[1]))
    acc_ref[...] += _dot(sc, w_ref[...])

    @pl.when(k == pl.num_programs(1) - 1)
    def _():
        o_ref[...] = acc_ref[0:1, :] + ml_ref[...]


def _modulation(c, mod_w, mod_layer):
    d, n = mod_w.shape
    nl = mod_layer.shape[0]
    tk = _pick(d, (1024, 512, 256, 128))
    tn = _pick(n, (2048, 1024, 512, 256, 128))
    ml = mod_layer.reshape(nl, n)
    return pl.pallas_call(
        _mod_kernel,
        out_shape=jax.ShapeDtypeStruct((nl, n), F32),
        grid=(n // tn, d // tk),
        in_specs=[
            pl.BlockSpec((1, tk), lambda j, k: (0, k)),
            pl.BlockSpec((tk, tn), lambda j, k: (k, j)),
            pl.BlockSpec((nl, tn), lambda j, k: (0, j)),
        ],
        out_specs=pl.BlockSpec((nl, tn), lambda j, k: (0, j)),
        scratch_shapes=[pltpu.VMEM((8, tn), F32)],
        compiler_params=_cparams(("parallel", "arbitrary")),
        name="adaln_modulation",
    )(c, mod_w, ml)


def _inproj_kernel(x_ref, nw_ref, sc_ref, sh_ref, w_ref, o_ref, h_ref):
    @pl.when(pl.program_id(1) == 0)
    def _():
        y = _rms(x_ref[...], nw_ref[...])
        h_ref[...] = (y * (1.0 + sc_ref[...]) + sh_ref[...]).astype(BF16)

    o_ref[...] = _dot(h_ref[...], w_ref[...])


def _in_projection(x, norm_w, sc, sh, w_bf16):
    s, d = x.shape
    n = w_bf16.shape[1]
    tm = _pick(s, (512, 256, 128))
    tn = _pick(n, (768, 512, 256, 128))
    row = lambda i, j: (0, 0)
    return pl.pallas_call(
        _inproj_kernel,
        out_shape=jax.ShapeDtypeStruct((s, n), F32),
        grid=(s // tm, n // tn),
        in_specs=[
            pl.BlockSpec((tm, d), lambda i, j: (i, 0)),
            pl.BlockSpec((1, d), row),
            pl.BlockSpec((1, d), row),
            pl.BlockSpec((1, d), row),
            pl.BlockSpec((d, tn), lambda i, j: (0, j)),
        ],
        out_specs=pl.BlockSpec((tm, tn), lambda i, j: (i, j)),
        scratch_shapes=[pltpu.VMEM((tm, d), BF16)],
        compiler_params=_cparams(("parallel", "arbitrary")),
        name="in_projection",
    )(x, norm_w, sc, sh, w_bf16)


def _causal_conv(buf_ref, x, cw_ref):
    t = x.shape[0]
    buf_ref[8:t + 8, :] = x
    y = cw_ref[CONV_W - 1:CONV_W, :] * x
    for j in range(CONV_W - 1):
        off = 8 - (CONV_W - 1) + j
        y = y + cw_ref[j:j + 1, :] * buf_ref[off:off + t, :]
    buf_ref[0:8, :] = buf_ref[t:t + 8, :]
    return y


def _shift_rows(x, d, fill, row_ids):
    return jnp.where(row_ids >= d, pltpu.roll(x, d, axis=0), fill)


def _rglru_kernel(xr_ref, xg_ref, cw_ref, cb_ref, wa_ref, ba_ref, wx_ref, bx_ref,
                  lam_ref, nw_ref, o_ref, xbuf, hprev, *, heads, head_dim):
    t = xr_ref.shape[0]

    @pl.when(pl.program_id(0) == 0)
    def _():
        xbuf[0:8, :] = jnp.zeros((8, xbuf.shape[1]), F32)
        hprev[...] = jnp.zeros_like(hprev)

    u = _causal_conv(xbuf, xr_ref[...], cw_ref) + cb_ref[...]
    ub = u.astype(BF16)
    ra, rx = [], []
    for h in range(heads):
        uh = ub[:, h * head_dim:(h + 1) * head_dim]
        ra.append(_dot(uh, wa_ref[h]))
        rx.append(_dot(uh, wx_ref[h]))
    r = _sigmoid(jnp.concatenate(ra, axis=1) + ba_ref[...])
    gate_i = _sigmoid(jnp.concatenate(rx, axis=1) + bx_ref[...])
    log_a = (-LRU_C) * r * _softplus(-lam_ref[...])
    a = jnp.exp(log_a)
    one_minus_a2 = jnp.tanh(-log_a) * (a * a + 1.0)
    b = jnp.sqrt(jnp.maximum(one_minus_a2, 0.0)) * (gate_i * u)

    row_ids = lax.broadcasted_iota(jnp.int32, a.shape, 0)
    d = 1
    while d < t:
        a_sh = _shift_rows(a, d, 1.0, row_ids)
        b_sh = _shift_rows(b, d, 0.0, row_ids)
        b = a * b_sh + b
        a = a * a_sh
        d *= 2
    hs = b + a * hprev[...]
    hprev[...] = hs[t - 1:t, :]

    y = _gelu_tanh(xg_ref[...]) * hs
    o_ref[...] = _rms(y, nw_ref[...]).astype(BF16)


def _rglru_group(p, col_rec, col_gate, conv_w, conv_b, wa, ba, wx, bx, lam, norm_w):
    s = p.shape[0]
    heads, head_dim = wa.shape[0], wa.shape[1]
    width = heads * head_dim
    t = _pick(s, (256, 128))
    vec = lambda v: v.reshape(1, width)
    full = lambda i: (0, 0)
    return pl.pallas_call(
        functools.partial(_rglru_kernel, heads=heads, head_dim=head_dim),
        out_shape=jax.ShapeDtypeStruct((s, width), BF16),
        grid=(s // t,),
        in_specs=[
            pl.BlockSpec((t, width), lambda i: (i, col_rec // width)),
            pl.BlockSpec((t, width), lambda i: (i, col_gate // width)),
            pl.BlockSpec((CONV_W, width), full),
            pl.BlockSpec((1, width), full),
            pl.BlockSpec((heads, head_dim, head_dim), lambda i: (0, 0, 0)),
            pl.BlockSpec((1, width), full),
            pl.BlockSpec((heads, head_dim, head_dim), lambda i: (0, 0, 0)),
            pl.BlockSpec((1, width), full),
            pl.BlockSpec((1, width), full),
            pl.BlockSpec((1, width), full),
        ],
        out_specs=pl.BlockSpec((t, width), lambda i: (i, 0)),
        scratch_shapes=[pltpu.VMEM((t + 8, width), F32), pltpu.VMEM((1, width), F32)],
        compiler_params=_cparams(("arbitrary",)),
        name="rglru_group",
    )(p, p, conv_w, vec(conv_b), wa.astype(BF16), vec(ba), wx.astype(BF16), vec(bx),
      vec(lam), vec(norm_w))


def _deltanet_kernel(q_ref, k_ref, v_ref, z_ref, g_ref, cwq_ref, cwk_ref, cwv_ref,
                     alog_ref, dtb_ref, nw_ref, o_ref, qbuf, kbuf, vbuf, state,
                     *, n_heads):
    h = pl.program_id(0)
    t, dh = q_ref.shape
    c = DN_CHUNK

    @pl.when(pl.program_id(1) == 0)
    def _():
        z8 = jnp.zeros((8, dh), F32)
        qbuf[0:8, :] = z8
        kbuf[0:8, :] = z8
        vbuf[0:8, :] = z8
        state[...] = jnp.zeros_like(state)

    xq = _silu(_causal_conv(qbuf, q_ref[...], cwq_ref))
    xk = _silu(_causal_conv(kbuf, k_ref[...], cwk_ref))
    xv = _silu(_causal_conv(vbuf, v_ref[...], cwv_ref))
    q = xq * lax.rsqrt(jnp.sum(xq * xq, axis=-1, keepdims=True) + EPS) * (dh ** -0.5)
    k = xk * lax.rsqrt(jnp.sum(xk * xk, axis=-1, keepdims=True) + EPS)

    gates = g_ref[...]
    lane = lax.broadcasted_iota(jnp.int32, gates.shape, 1)
    beta_raw = jnp.sum(jnp.where(lane == h, gates, 0.0), axis=-1, keepdims=True)
    alpha_raw = jnp.sum(jnp.where(lane == h + n_heads, gates, 0.0), axis=-1, keepdims=True)
    beta = jnp.broadcast_to(_sigmoid(beta_raw), (t, dh))
    g = jnp.broadcast_to(alpha_raw, (t, dh))
    g = -jnp.exp(alog_ref[...]) * _softplus(g + dtb_ref[...])

    assert dh == c, "decay matrix is built from a (chunk, head_dim) lane-replicated tile"
    rows = lax.broadcasted_iota(jnp.int32, (c, c), 0)
    cols = lax.broadcasted_iota(jnp.int32, (c, c), 1)
    lower = rows >= cols
    strict = rows > cols
    eye = jnp.where(rows == cols, 1.0, 0.0)
    blk = []
    b = DN_BASE
    while b <= c:
        sh = int(math.log2(b))
        blk.append((rows >> sh) == (cols >> sh))
        b *= 2

    pre = []
    for ci in range(t // c):
        sl = slice(ci * c, (ci + 1) * c)
        qc, kc, vc, bc = q[sl], k[sl], xv[sl], beta[sl]
        gc = g[sl]
        d = 1
        while d < c:
            gc = gc + _shift_rows(gc, d, 0.0, rows)
            d *= 2
        gct = jnp.transpose(gc)
        decay = jnp.where(lower, jnp.exp(jnp.where(lower, gc - gct, 0.0)), 0.0)
        eg = jnp.exp(gc)
        g_last = gc[c - 1:c, :]
        kb = kc * bc
        kb16, k16 = kb.astype(BF16), kc.astype(BF16)
        lmat = jnp.where(strict, _dot_nt(kb16, k16) * decay, 0.0)
        pk = (-jnp.where(blk[0], lmat, 0.0)).astype(BF16)
        tinv = eye + pk
        for _ in range(int(math.log2(DN_BASE)) - 1):
            pk = _dot(pk, pk).astype(BF16)
            tinv = tinv + _dot(tinv.astype(BF16), pk)
        for lvl in range(len(blk) - 1):
            l_off = jnp.where(blk[lvl + 1], jnp.where(blk[lvl], 0.0, lmat), 0.0).astype(BF16)
            t16 = tinv.astype(BF16)
            tinv = tinv - _dot(_dot(t16, l_off).astype(BF16), t16)
        rhs = jnp.concatenate([vc * bc, kb * eg], axis=1)
        x = _dot(tinv.astype(BF16), rhs.astype(BF16))
        u, w = x[:, :dh], x[:, dh:]
        qk = _dot_nt(qc.astype(BF16), k16) * decay
        q_dec = (qc * eg).astype(BF16)
        k_tail = kc * jnp.exp(g_last - gc)
        pre.append((u, w.astype(BF16), qk.astype(BF16), q_dec,
                    jnp.transpose(k_tail).astype(BF16), jnp.exp(g_last)))

    st = state[...]
    outs = []
    for u, w16, qk16, q_dec, k_tail_t, gl in pre:
        st16 = st.astype(BF16)
        v_new = u - _dot(w16, st16)
        v16 = v_new.astype(BF16)
        outs.append(_dot(q_dec, st16) + _dot(qk16, v16))
        st = st * gl + _dot(k_tail_t, v16)
    state[...] = st

    o = jnp.concatenate(outs, axis=0)
    o = _rms(o, nw_ref[...]) * _silu(z_ref[...])
    o_ref[...] = o.astype(BF16)


def _deltanet_group(p, cols, col_gates, conv_w, a_log, dt_bias, norm_w):
    s = p.shape[0]
    n_heads = a_log.shape[0]
    dh = norm_w.shape[0]
    width = n_heads * dh
    col_q, col_k, col_v, col_z = cols
    t = _pick(s, (512, 256, 128))
    lanes = lambda v: jnp.broadcast_to(v.reshape(n_heads, 1, 1), (n_heads, 1, dh))

    def pcol(col):
        return pl.BlockSpec((t, dh), lambda h, i: (i, col // dh + h))

    def ccol(grp):
        return pl.BlockSpec((CONV_W, dh), lambda h, i: (0, grp * n_heads + h))

    return pl.pallas_call(
        functools.partial(_deltanet_kernel, n_heads=n_heads),
        out_shape=jax.ShapeDtypeStruct((s, width), BF16),
        grid=(n_heads, s // t),
        in_specs=[
            pcol(col_q), pcol(col_k), pcol(col_v), pcol(col_z),
            pl.BlockSpec((t, LANES), lambda h, i: (i, col_gates // LANES)),
            ccol(0), ccol(1), ccol(2),
            pl.BlockSpec((None, 1, dh), lambda h, i: (h, 0, 0)),
            pl.BlockSpec((None, 1, dh), lambda h, i: (h, 0, 0)),
            pl.BlockSpec((1, dh), lambda h, i: (0, 0)),
        ],
        out_specs=pl.BlockSpec((t, dh), lambda h, i: (i, h)),
        scratch_shapes=[pltpu.VMEM((t + 8, dh), F32)] * 3 + [pltpu.VMEM((dh, dh), F32)],
        compiler_params=_cparams(("parallel", "arbitrary")),
        name="deltanet_group",
    )(p, p, p, p, p, conv_w, conv_w, conv_w, lanes(a_log), lanes(dt_bias),
      norm_w.reshape(1, dh))


def _mla_proj_kernel(cq_ref, ckv_ref, ckr_ref, pos_ref, inv_ref, qn_ref, kvn_ref,
                     wq_ref, wkv_ref, q_out, k_out, v_out, hq, hkv, cos_t, sin_t, kpe,
                     *, scale):
    @pl.when(pl.program_id(1) == 0)
    def _():
        hq[...] = _rms(cq_ref[...], qn_ref[...]).astype(BF16)
        hkv[...] = _rms(ckv_ref[...], kvn_ref[...]).astype(BF16)
        ang = pos_ref[...].astype(F32) * inv_ref[...]
        lane = lax.broadcasted_iota(jnp.int32, ang.shape, 1)
        cs = jnp.where(lane < C_ROPE, jnp.cos(ang), 0.0)
        sn = jnp.where(lane < C_ROPE, jnp.sin(ang), 0.0)
        cos_t[...] = cs
        sin_t[...] = sn
        r2 = ckr_ref[...]
        kpe[...] = (r2 * cs + pltpu.roll(r2, C_ROPE, axis=1) * sn).astype(BF16)

    rq = _dot(hq[...], wq_ref[...])
    r2 = rq[:, C_NOPE:]
    q_pe = r2 * cos_t[...] + pltpu.roll(r2, C_ROPE, axis=1) * sin_t[...]
    q_out[:, :C_NOPE] = (rq[:, :C_NOPE] * scale).astype(BF16)
    q_out[:, C_NOPE:] = (q_pe * scale).astype(BF16)

    rkv = _dot(hkv[...], wkv_ref[...])
    k_out[:, :C_NOPE] = rkv[:, :C_NOPE].astype(BF16)
    k_out[:, C_NOPE:] = kpe[...]
    v_out[...] = rkv[:, C_NOPE:].astype(BF16)


def _rope_rot_cols(w):
    half = C_ROPE // 2
    return jnp.concatenate([-w[..., half:], w[..., :half]], axis=-1)


def _mla_projections(p, col_q, col_kv, col_kr, positions, q_norm_w, w_q_up, kv_norm_w, w_kv_up):
    s = p.shape[0]
    q_lora, n_heads, dqk = w_q_up.shape
    kv_lora = w_kv_up.shape[0]
    dv = w_kv_up.shape[2] - C_NOPE
    assert dqk == C_NOPE + C_ROPE and 2 * C_ROPE == LANES
    t = _pick(s, (512, 256, 128))
    w_pe = w_q_up[..., C_NOPE:]
    wq = jnp.concatenate([w_q_up[..., :C_NOPE], w_pe, _rope_rot_cols(w_pe)], axis=-1)
    wq = jnp.transpose(wq, (1, 0, 2)).astype(BF16)
    wkv = jnp.transpose(w_kv_up, (1, 0, 2)).astype(BF16)
    inv = ROPE_THETA ** (-jnp.arange(0, C_ROPE, 2, dtype=F32) / C_ROPE)
    inv = jnp.concatenate([inv, inv, jnp.zeros((LANES - C_ROPE,), F32)]).reshape(1, LANES)
    dk = C_NOPE + LANES
    scale = float(dqk) ** -0.5
    full = lambda i, h: (0, 0)
    return pl.pallas_call(
        functools.partial(_mla_proj_kernel, scale=scale),
        out_shape=(jax.ShapeDtypeStruct((n_heads, s, dk), BF16),
                   jax.ShapeDtypeStruct((n_heads, s, dk), BF16),
                   jax.ShapeDtypeStruct((n_heads, s, dv), BF16)),
        grid=(s // t, n_heads),
        in_specs=[
            pl.BlockSpec((t, q_lora), lambda i, h: (i, col_q // q_lora)),
            pl.BlockSpec((t, kv_lora), lambda i, h: (i, col_kv // kv_lora)),
            pl.BlockSpec((t, LANES), lambda i, h: (i, col_kr // LANES)),
            pl.BlockSpec((t, 1), lambda i, h: (i, 0)),
            pl.BlockSpec((1, LANES), full),
            pl.BlockSpec((1, q_lora), full),
            pl.BlockSpec((1, kv_lora), full),
            pl.BlockSpec((None, q_lora, dk), lambda i, h: (h, 0, 0)),
            pl.BlockSpec((None, kv_lora, C_NOPE + dv), lambda i, h: (h, 0, 0)),
        ],
        out_specs=(pl.BlockSpec((None, t, dk), lambda i, h: (h, i, 0)),
                   pl.BlockSpec((None, t, dk), lambda i, h: (h, i, 0)),
                   pl.BlockSpec((None, t, dv), lambda i, h: (h, i, 0))),
        scratch_shapes=[pltpu.VMEM((t, q_lora), BF16), pltpu.VMEM((t, kv_lora), BF16),
                        pltpu.VMEM((t, LANES), F32), pltpu.VMEM((t, LANES), F32),
                        pltpu.VMEM((t, LANES), BF16)],
        compiler_params=_cparams(("parallel", "arbitrary")),
        name="mla_projections",
    )(p, p, p, positions.reshape(s, 1), inv, q_norm_w.reshape(1, q_lora),
      kv_norm_w.reshape(1, kv_lora), wq, wkv)


def _flash_kernel(q_ref, k_ref, v_ref, o_ref, m_ref, l_ref, acc_ref):
    i, j = pl.program_id(1), pl.program_id(2)
    tq, tk = q_ref.shape[0], k_ref.shape[0]

    @pl.when(j == 0)
    def _():
        m_ref[...] = jnp.full_like(m_ref, NEG_BIG)
        l_ref[...] = jnp.zeros_like(l_ref)
        acc_ref[...] = jnp.zeros_like(acc_ref)

    def update(masked):
        s = _dot_nt(q_ref[...], k_ref[...])
        if masked:
            rows = lax.broadcasted_iota(jnp.int32, s.shape, 0)
            cols = lax.broadcasted_iota(jnp.int32, s.shape, 1)
            s = jnp.where(cols <= rows, s, NEG_BIG)
        m_prev = m_ref[...]
        m_next = jnp.maximum(m_prev, jnp.max(s, axis=1, keepdims=True))
        alpha = jnp.exp(m_prev - m_next)
        pexp = jnp.exp(s - m_next)
        l_ref[...] = alpha * l_ref[...] + jnp.sum(pexp, axis=1, keepdims=True)
        acc_ref[...] = alpha * acc_ref[...] + _dot(pexp.astype(BF16), v_ref[...])
        m_ref[...] = m_next

    @pl.when(j < i)
    def _():
        update(False)

    @pl.when(j == i)
    def _():
        update(True)
        o_ref[...] = (acc_ref[...] / l_ref[...]).astype(o_ref.dtype)


def _causal_attention(q, k, v):
    n_heads, s, dk = q.shape
    dv = v.shape[2]
    t = _pick(s, (512, 256, 128))
    n = s // t
    return pl.pallas_call(
        _flash_kernel,
        out_shape=jax.ShapeDtypeStruct((s, n_heads * dv), BF16),
        grid=(n_heads, n, n),
        in_specs=[
            pl.BlockSpec((None, t, dk), lambda h, i, j: (h, i, 0)),
            pl.BlockSpec((None, t, dk), lambda h, i, j: (h, jnp.minimum(j, i), 0)),
            pl.BlockSpec((None, t, dv), lambda h, i, j: (h, jnp.minimum(j, i), 0)),
        ],
        out_specs=pl.BlockSpec((t, dv), lambda h, i, j: (i, h)),
        scratch_shapes=[pltpu.VMEM((t, 1), F32), pltpu.VMEM((t, 1), F32),
                        pltpu.VMEM((t, dv), F32)],
        compiler_params=_cparams(("parallel", "parallel", "arbitrary")),
        name="mla_flash_attention",
    )(q, k, v)


def _outproj_kernel(ya_ref, yb_ref, yc_ref, nc_ref, w_ref, x_ref, g_ref, o_ref, ycn):
    wa, wb = ya_ref.shape[1], yb_ref.shape[1]

    @pl.when(pl.program_id(1) == 0)
    def _():
        ycn[...] = _rms(yc_ref[...].astype(F32), nc_ref[...]).astype(BF16)

    acc = _dot(ya_ref[...], w_ref[0:wa, :])
    acc += _dot(yb_ref[...], w_ref[wa:wa + wb, :])
    acc += _dot(ycn[...], w_ref[wa + wb:, :])
    o_ref[...] = x_ref[...] + g_ref[...] * acc


def _out_projection(ya, yb, yc, norm_c, w_bf16, x, gate):
    s, d = x.shape
    wa, wb, wc = ya.shape[1], yb.shape[1], yc.shape[1]
    tm = _pick(s, (512, 256, 128))
    tn = _pick(d, (1024, 512, 256, 128))
    return pl.pallas_call(
        _outproj_kernel,
        out_shape=jax.ShapeDtypeStruct((s, d), F32),
        grid=(s // tm, d // tn),
        in_specs=[
            pl.BlockSpec((tm, wa), lambda i, j: (i, 0)),
            pl.BlockSpec((tm, wb), lambda i, j: (i, 0)),
            pl.BlockSpec((tm, wc), lambda i, j: (i, 0)),
            pl.BlockSpec((1, wc), lambda i, j: (0, 0)),
            pl.BlockSpec((wa + wb + wc, tn), lambda i, j: (0, j)),
            pl.BlockSpec((tm, tn), lambda i, j: (i, j)),
            pl.BlockSpec((1, tn), lambda i, j: (0, j)),
        ],
        out_specs=pl.BlockSpec((tm, tn), lambda i, j: (i, j)),
        scratch_shapes=[pltpu.VMEM((tm, wc), BF16)],
        compiler_params=_cparams(("parallel", "arbitrary")),
        name="out_projection",
    )(ya, yb, yc, norm_c.reshape(1, wc), w_bf16, x, gate)


def _take_top(work, dst_ref, n):
    for kk in range(n):
        m = jnp.max(work, axis=0, keepdims=True)
        dst_ref[kk:kk + 1, :] = m
        work = jnp.where(work == m, -jnp.inf, work)


def _peer_route_kernel(x_ref, nw_ref, sc_ref, sh_ref, wq_ref, keys_ref,
                       h_out, st_out, tau_out, lse_out, h_scr, top1, top2, cand, best):
    @pl.when(pl.program_id(1) == 0)
    def _():
        y = _rms(x_ref[...], nw_ref[...])
        hb = (y * (1.0 + sc_ref[...]) + sh_ref[...]).astype(BF16)
        h_scr[...] = hb
        h_out[...] = hb

    half = keys_ref.shape[2]
    q16 = _dot(h_scr[...], wq_ref[...]).astype(BF16)
    for part, top in ((0, top1), (1, top2)):
        s_t = _dot_nt(keys_ref[part], q16[:, part * half:(part + 1) * half])
        st_out[part] = s_t
        _take_top(s_t, top, PEER_TOPK)

    k = PEER_TOPK
    cand[0:k, :] = top1[0:1, :] + top2[0:k, :]
    for i in range(1, k // 2):
        cand[k + (i - 1) * (k // 2):k + i * (k // 2), :] = top1[i:i + 1, :] + top2[0:k // 2, :]
    base = k + (k // 2 - 1) * (k // 2)
    cand[base:base + k // 2, :] = top1[k // 2:k, :] + top2[0:1, :]
    _take_top(cand[...], best, k)

    b = best[...]
    m0 = b[0:1, :]
    z = jnp.sum(jnp.exp(b - m0), axis=0, keepdims=True)
    tau_out[...] = b[k - 1:k, :]
    lse_out[...] = m0 + jnp.log(z)


def _peer_route(x, norm_w, sc, sh, wq_bf16, keys_bf16):
    s, d = x.shape
    n_heads, _, n_keys, half = keys_bf16.shape
    assert PEER_TOPK == 16 and n_keys % 8 == 0
    tm = _pick(s, (512, 256, 128))
    n_cand = PEER_TOPK + (PEER_TOPK // 2) * (PEER_TOPK // 2)
    row = lambda i, h: (0, 0)
    return pl.pallas_call(
        _peer_route_kernel,
        out_shape=(jax.ShapeDtypeStruct((s, d), BF16),
                   jax.ShapeDtypeStruct((n_heads, 2, n_keys, s), F32),
                   jax.ShapeDtypeStruct((n_heads, 1, s), F32),
                   jax.ShapeDtypeStruct((n_heads, 1, s), F32)),
        grid=(s // tm, n_heads),
        in_specs=[
            pl.BlockSpec((tm, d), lambda i, h: (i, 0)),
            pl.BlockSpec((1, d), row),
            pl.BlockSpec((1, d), row),
            pl.BlockSpec((1, d), row),
            pl.BlockSpec((d, 2 * half), lambda i, h: (0, h)),
            pl.BlockSpec((None, 2, n_keys, half), lambda i, h: (h, 0, 0, 0)),
        ],
        out_specs=(pl.BlockSpec((tm, d), lambda i, h: (i, 0)),
                   pl.BlockSpec((None, 2, n_keys, tm), lambda i, h: (h, 0, 0, i)),
                   pl.BlockSpec((None, 1, tm), lambda i, h: (h, 0, i)),
                   pl.BlockSpec((None, 1, tm), lambda i, h: (h, 0, i))),
        scratch_shapes=[pltpu.VMEM((tm, d), BF16),
                        pltpu.VMEM((PEER_TOPK, tm), F32), pltpu.VMEM((PEER_TOPK, tm), F32),
                        pltpu.VMEM((n_cand, tm), F32), pltpu.VMEM((PEER_TOPK, tm), F32)],
        compiler_params=_cparams(("parallel", "arbitrary")),
        name="peer_route",
    )(x, norm_w, sc, sh, wq_bf16, keys_bf16)


def _peer_dense_kernel(h_ref, u_ref, vt_ref, st_ref, tau_ref, lse_ref, acc_ref):
    j = pl.program_id(1)
    n_heads, _, n_keys, tm = st_ref.shape
    tn = u_ref.shape[0]

    @pl.when(j == 0)
    def _():
        acc_ref[...] = jnp.zeros_like(acc_ref)

    act = _gelu_tanh(_dot_nt(u_ref[...], h_ref[...]))
    tau = tau_ref[...]
    lse = lse_ref[...]
    pieces = []
    for ab in range(tn // n_keys):
        a = j * (tn // n_keys) + ab
        w = jnp.zeros((n_keys, tm), F32)
        for hd in range(n_heads):
            x = st_ref[hd, 1] + st_ref[hd, 0, pl.ds(a, 1), :]
            w = w + jnp.where(x >= tau[hd:hd + 1, :], jnp.exp(x - lse[hd:hd + 1, :]), 0.0)
        pieces.append((w * act[ab * n_keys:(ab + 1) * n_keys, :]).astype(BF16))
    coef = pieces[0] if len(pieces) == 1 else jnp.concatenate(pieces, axis=0)
    acc_ref[...] += _dot(vt_ref[...], coef)


def _peer_dense(h, u_bf16, vt_bf16, st, tau, lse):
    s, d = h.shape
    n_exp = u_bf16.shape[0]
    n_heads, _, n_keys, _ = st.shape
    tm = _pick(s, (512, 256, 128))
    tn = _pick(n_exp, (512, 256, 128))
    assert tn % n_keys == 0
    return pl.pallas_call(
        _peer_dense_kernel,
        out_shape=jax.ShapeDtypeStruct((d, s), F32),
        grid=(s // tm, n_exp // tn),
        in_specs=[
            pl.BlockSpec((tm, d), lambda i, j: (i, 0)),
            pl.BlockSpec((tn, d), lambda i, j: (j, 0)),
            pl.BlockSpec((d, tn), lambda i, j: (0, j)),
            pl.BlockSpec((n_heads, 2, n_keys, tm), lambda i, j: (0, 0, 0, i)),
            pl.BlockSpec((n_heads, tm), lambda i, j: (0, i)),
            pl.BlockSpec((n_heads, tm), lambda i, j: (0, i)),
        ],
        out_specs=pl.BlockSpec((d, tm), lambda i, j: (0, i)),
        compiler_params=_cparams(("parallel", "arbitrary")),
        name="peer_dense",
    )(h, u_bf16, vt_bf16, st, tau, lse)


def _peer_residual_kernel(acc_ref, x_ref, g_ref, fw_ref, o_ref, *, final_norm):
    y = x_ref[...] + g_ref[...] * jnp.transpose(acc_ref[...])
    if final_norm:
        y = _rms(y, fw_ref[...])
    o_ref[...] = y


def _peer_residual(acc_t, x, gate, final_w, final_norm):
    s, d = x.shape
    tm = _pick(s, (256, 128))
    row = lambda i: (0, 0)
    return pl.pallas_call(
        functools.partial(_peer_residual_kernel, final_norm=final_norm),
        out_shape=jax.ShapeDtypeStruct((s, d), F32),
        grid=(s // tm,),
        in_specs=[
            pl.BlockSpec((d, tm), lambda i: (0, i)),
            pl.BlockSpec((tm, d), lambda i: (i, 0)),
            pl.BlockSpec((1, d), row),
            pl.BlockSpec((1, d), row),
        ],
        out_specs=pl.BlockSpec((tm, d), lambda i: (i, 0)),
        compiler_params=_cparams(("parallel",)),
        name="peer_residual",
    )(acc_t, x, gate, final_w.reshape(1, d))


def _pack_in_weights(w_in, a_width, b_width, b_heads, q_lora, kv_lora):
    o_beta = 2 * a_width + 4 * b_width
    o_alpha = o_beta + b_heads
    o_q = o_alpha + b_heads
    o_kv = o_q + q_lora
    o_kr = o_kv + kv_lora
    d = w_in.shape[0]
    w_kr = w_in[:, o_kr:o_kr + C_ROPE]
    n_gate_pad = LANES - 2 * b_heads
    parts = [w_in[:, :o_beta], w_in[:, o_q:o_kv], w_in[:, o_kv:o_kr], w_kr, _rope_rot_cols(w_kr),
             w_in[:, o_beta:o_q], jnp.zeros((d, n_gate_pad), w_in.dtype)]
    cols = dict(rec=0, gate=a_width, q=2 * a_width, k=2 * a_width + b_width,
                v=2 * a_width + 2 * b_width, z=2 * a_width + 3 * b_width,
                cq=o_beta, ckv=o_beta + q_lora, ckr=o_beta + q_lora + kv_lora,
                gates=o_beta + q_lora + kv_lora + LANES)
    n = cols["gates"] + LANES
    n_pad = -n % 768 if n >= 768 else 0
    if n_pad:
        parts.append(jnp.zeros((d, n_pad), w_in.dtype))
    return jnp.concatenate(parts, axis=1).astype(BF16), cols


def kernel(x, c, positions, mod_w, mod_layer, norm_mix_w, w_in, lru_conv_w, lru_conv_b, lru_wa, lru_ba, lru_wx, lru_bx, lru_lambda, dn_conv_w, dn_a_log, dn_dt_bias, dn_norm_w, mla_q_norm_w, mla_w_q_up, mla_kv_norm_w, mla_w_kv_up, branch_norm_a, branch_norm_c, w_out, norm_ffn_w, peer_w_query, peer_sub_keys, peer_u, peer_v, final_norm_w):
    bsz, s, d = x.shape
    assert bsz == 1, "kernels are written for a single sequence"
    depth = w_in.shape[0]
    a_width = lru_conv_w.shape[2]
    b_width = dn_conv_w.shape[2] // 3
    b_heads = dn_a_log.shape[1]
    q_lora = mla_q_norm_w.shape[1]
    kv_lora = mla_kv_norm_w.shape[1]
    p_heads, q_dim = peer_w_query.shape[2], peer_w_query.shape[3]

    mod = _modulation(c, mod_w, mod_layer).reshape(depth, N_MOD, 1, d)
    xs = x.reshape(s, d)
    pos = positions.reshape(s)

    for l in range(depth):
        sh_a, sc_a, g_a, sh_f, sc_f, g_f = [mod[l, i] for i in range(N_MOD)]
        w_in_p, cols = _pack_in_weights(w_in[l], a_width, b_width, b_heads, q_lora, kv_lora)
        p = _in_projection(xs, norm_mix_w[l].reshape(1, d), sc_a, sh_a, w_in_p)

        ya = _rglru_group(p, cols["rec"], cols["gate"], lru_conv_w[l], lru_conv_b[l], lru_wa[l],
                          lru_ba[l], lru_wx[l], lru_bx[l], lru_lambda[l], branch_norm_a[l])
        yb = _deltanet_group(p, (cols["q"], cols["k"], cols["v"], cols["z"]), cols["gates"],
                             dn_conv_w[l], dn_a_log[l], dn_dt_bias[l], dn_norm_w[l])
        q, k, v = _mla_projections(p, cols["cq"], cols["ckv"], cols["ckr"], pos, mla_q_norm_w[l],
                                   mla_w_q_up[l], mla_kv_norm_w[l], mla_w_kv_up[l])
        yc = _causal_attention(q, k, v)
        xs = _out_projection(ya, yb, yc, branch_norm_c[l], w_out[l].astype(BF16), xs, g_a)

        wq = peer_w_query[l].reshape(d, p_heads * q_dim).astype(BF16)
        h2, st, tau, lse = _peer_route(xs, norm_ffn_w[l].reshape(1, d), sc_f, sh_f, wq,
                                       peer_sub_keys[l].astype(BF16))
        acc_t = _peer_dense(h2, peer_u[l].astype(BF16), jnp.transpose(peer_v[l]).astype(BF16), st,
                            tau.reshape(p_heads, s), lse.reshape(p_heads, s))
        xs = _peer_residual(acc_t, xs, g_f, final_norm_w, final_norm=(l == depth - 1))
    return xs.reshape(bsz, s, d)
```

```python
import functools
import math

import jax
import jax.numpy as jnp
from jax import lax
from jax.experimental import pallas as pl
from jax.experimental.pallas import tpu as pltpu

F32 = jnp.float32
BF16 = jnp.bfloat16
EPS = 1e-6
N_MOD = 6
LRU_C = 8.0
CONV_W = 4
C_NOPE = 128
C_ROPE = 64
ROPE_THETA = 10000.0
PEER_TOPK = 16
ATTN_TK = 512
DN_CHUNK = 128
DN_BASE = 16
LANES = 128
MXU_WIDTH = 256
NEG_BIG = -1e30

_VMEM_LIMIT = 56 * 1024 * 1024


def _cparams(sem, vmem=_VMEM_LIMIT):
    return pltpu.CompilerParams(dimension_semantics=sem, vmem_limit_bytes=vmem)


def _pick(n, candidates):
    for c in candidates:
        if n % c == 0:
            return c
    raise ValueError(f"no tile in {candidates} divides {n}")


def _sigmoid(x):
    return 1.0 / (1.0 + jnp.exp(-x))


def _softplus(x):
    return jnp.maximum(x, 0.0) + jnp.log1p(jnp.exp(-jnp.abs(x)))


def _silu(x):
    return x * _sigmoid(x)


def _gelu_tanh(x):
    c = math.sqrt(2.0 / math.pi)
    return 0.5 * x * (1.0 + jnp.tanh(c * (x + 0.044715 * (x * x * x))))


def _rms(x, w):
    ms = jnp.mean(x * x, axis=-1, keepdims=True)
    return x * lax.rsqrt(ms + EPS) * w


def _dot(a, b):
    return jnp.dot(a, b, preferred_element_type=F32)


def _dot_nt(a, b):
    return lax.dot_general(a, b, (((1,), (1,)), ((), ())), preferred_element_type=F32)


def _mod_kernel(c_ref, w_ref, ml_ref, o_ref, acc_ref):
    k = pl.program_id(1)

    @pl.when(k == 0)
    def _():
        acc_ref[...] = jnp.zeros_like(acc_ref)

    c = c_ref[...]
    sc = jnp.broadcast_to(_silu(c), (8, c.shape[1]))
    acc_ref[...] += _dot(sc, w_ref[...])

    @pl.when(k == pl.num_programs(1) - 1)
    def _():
        o_ref[...] = acc_ref[0:1, :] + ml_ref[...]


def _modulation(c, mod_w, mod_layer):
    d, n = mod_w.shape
    nl = mod_layer.shape[0]
    tk = _pick(d, (1024, 512, 256, 128))
    tn = _pick(n, (2048, 1024, 512, 256, 128))
    ml = mod_layer.reshape(nl, n)
    return pl.pallas_call(
        _mod_kernel,
        out_shape=jax.ShapeDtypeStruct((nl, n), F32),
        grid=(n // tn, d // tk),
        in_specs=[
            pl.BlockSpec((1, tk), lambda j, k: (0, k)),
            pl.BlockSpec((tk, tn), lambda j, k: (k, j)),
            pl.BlockSpec((nl, tn), lambda j, k: (0, j)),
        ],
        out_specs=pl.BlockSpec((nl, tn), lambda j, k: (0, j)),
        scratch_shapes=[pltpu.VMEM((8, tn), F32)],
        compiler_params=_cparams(("parallel", "arbitrary")),
        name="adaln_modulation",
    )(c, mod_w, ml)


def _inproj_kernel(x_ref, nw_ref, sc_ref, sh_ref, w_ref, o_ref, h_ref):
    @pl.when(pl.program_id(1) == 0)
    def _():
        y = _rms(x_ref[...], nw_ref[...])
        h_ref[...] = (y * (1.0 + sc_ref[...]) + sh_ref[...]).astype(BF16)

    o_ref[...] = _dot(h_ref[...], w_ref[...])


def _in_projection(x, norm_w, sc, sh, w_bf16):
    s, d = x.shape
    n = w_bf16.shape[1]
    tm = _pick(s, (512, 256, 128))
    tn = _pick(n, (768, 512, 256, 128))
    row = lambda i, j: (0, 0)
    return pl.pallas_call(
        _inproj_kernel,
        out_shape=jax.ShapeDtypeStruct((s, n), F32),
        grid=(s // tm, n // tn),
        in_specs=[
            pl.BlockSpec((tm, d), lambda i, j: (i, 0)),
            pl.BlockSpec((1, d), row),
            pl.BlockSpec((1, d), row),
            pl.BlockSpec((1, d), row),
            pl.BlockSpec((d, tn), lambda i, j: (0, j)),
        ],
        out_specs=pl.BlockSpec((tm, tn), lambda i, j: (i, j)),
        scratch_shapes=[pltpu.VMEM((tm, d), BF16)],
        compiler_params=_cparams(("parallel", "arbitrary")),
        name="in_projection",
    )(x, norm_w, sc, sh, w_bf16)


def _causal_conv(buf_ref, x, cw_ref):
    t = x.shape[0]
    buf_ref[8:t + 8, :] = x
    y = cw_ref[CONV_W - 1:CONV_W, :] * x
    for j in range(CONV_W - 1):
        off = 8 - (CONV_W - 1) + j
        y = y + cw_ref[j:j + 1, :] * buf_ref[off:off + t, :]
    buf_ref[0:8, :] = buf_ref[t:t + 8, :]
    return y


def _shift_rows(x, d, fill, row_ids):
    return jnp.where(row_ids >= d, pltpu.roll(x, d, axis=0), fill)


def _rglru_kernel(xr_ref, xg_ref, cw_ref, cb_ref, wa_ref, ba_ref, wx_ref, bx_ref,
                  lam_ref, nw_ref, o_ref, xbuf, hprev, *, heads, head_dim):
    t = xr_ref.shape[0]

    @pl.when(pl.program_id(0) == 0)
    def _():
        xbuf[0:8, :] = jnp.zeros((8, xbuf.shape[1]), F32)
        hprev[...] = jnp.zeros_like(hprev)

    u = _causal_conv(xbuf, xr_ref[...], cw_ref) + cb_ref[...]
    ub = u.astype(BF16)
    ra, rx = [], []
    for h in range(heads):
        uh = ub[:, h * head_dim:(h + 1) * head_dim]
        ra.append(_dot(uh, wa_ref[h]))
        rx.append(_dot(uh, wx_ref[h]))
    r = _sigmoid(jnp.concatenate(ra, axis=1) + ba_ref[...])
    gate_i = _sigmoid(jnp.concatenate(rx, axis=1) + bx_ref[...])
    log_a = (-LRU_C) * r * _softplus(-lam_ref[...])
    a = jnp.exp(log_a)
    one_minus_a2 = jnp.tanh(-log_a) * (a * a + 1.0)
    b = jnp.sqrt(jnp.maximum(one_minus_a2, 0.0)) * (gate_i * u)

    row_ids = lax.broadcasted_iota(jnp.int32, a.shape, 0)
    d = 1
    while d < t:
        a_sh = _shift_rows(a, d, 1.0, row_ids)
        b_sh = _shift_rows(b, d, 0.0, row_ids)
        b = a * b_sh + b
        a = a * a_sh
        d *= 2
    hs = b + a * hprev[...]
    hprev[...] = hs[t - 1:t, :]

    y = _gelu_tanh(xg_ref[...]) * hs
    o_ref[...] = _rms(y, nw_ref[...]).astype(BF16)


def _rglru_group(p, col_rec, col_gate, conv_w, conv_b, wa, ba, wx, bx, lam, norm_w):
    s = p.shape[0]
    heads, head_dim = wa.shape[0], wa.shape[1]
    width = heads * head_dim
    t = _pick(s, (256, 128))
    vec = lambda v: v.reshape(1, width)
    full = lambda i: (0, 0)
    return pl.pallas_call(
        functools.partial(_rglru_kernel, heads=heads, head_dim=head_dim),
        out_shape=jax.ShapeDtypeStruct((s, width), BF16),
        grid=(s // t,),
        in_specs=[
            pl.BlockSpec((t, width), lambda i: (i, col_rec // width)),
            pl.BlockSpec((t, width), lambda i: (i, col_gate // width)),
            pl.BlockSpec((CONV_W, width), full),
            pl.BlockSpec((1, width), full),
            pl.BlockSpec((heads, head_dim, head_dim), lambda i: (0, 0, 0)),
            pl.BlockSpec((1, width), full),
            pl.BlockSpec((heads, head_dim, head_dim), lambda i: (0, 0, 0)),
            pl.BlockSpec((1, width), full),
            pl.BlockSpec((1, width), full),
            pl.BlockSpec((1, width), full),
        ],
        out_specs=pl.BlockSpec((t, width), lambda i: (i, 0)),
        scratch_shapes=[pltpu.VMEM((t + 8, width), F32), pltpu.VMEM((1, width), F32)],
        compiler_params=_cparams(("arbitrary",)),
        name="rglru_group",
    )(p, p, conv_w, vec(conv_b), wa.astype(BF16), vec(ba), wx.astype(BF16), vec(bx),
      vec(lam), vec(norm_w))


def _deltanet_kernel(q_ref, k_ref, v_ref, z_ref, g_ref, cwq_ref, cwk_ref, cwv_ref,
                     alog_ref, dtb_ref, nw_ref, o_ref, qbuf, kbuf, vbuf, state,
                     *, n_heads, hp):
    hg = pl.program_id(0)
    t = q_ref.shape[0]
    dh = q_ref.shape[1] // hp
    c = DN_CHUNK
    n_ch = t // c
    assert dh == c, "decay matrix is built from a (chunk, head_dim) lane-replicated tile"

    @pl.when(pl.program_id(1) == 0)
    def _():
        z8 = jnp.zeros((8, hp * dh), F32)
        qbuf[0:8, :] = z8
        kbuf[0:8, :] = z8
        vbuf[0:8, :] = z8
        state[...] = jnp.zeros_like(state)

    xq = _silu(_causal_conv(qbuf, q_ref[...], cwq_ref))
    xk = _silu(_causal_conv(kbuf, k_ref[...], cwk_ref))
    xv = _silu(_causal_conv(vbuf, v_ref[...], cwv_ref))
    gates = g_ref[...]
    lane = lax.broadcasted_iota(jnp.int32, gates.shape, 1)

    rows = lax.broadcasted_iota(jnp.int32, (c, c), 0)
    cols = lax.broadcasted_iota(jnp.int32, (c, c), 1)
    lower = rows >= cols
    strict = rows > cols
    eye = jnp.where(rows == cols, 1.0, 0.0)
    blk = []
    b = DN_BASE
    while b <= c:
        sh = int(math.log2(b))
        blk.append((rows >> sh) == (cols >> sh))
        b *= 2

    pairs = []
    for hh in range(hp):
        h = hg * hp + hh
        hs = slice(hh * dh, (hh + 1) * dh)
        xqh, xkh = xq[:, hs], xk[:, hs]
        qh = xqh * lax.rsqrt(jnp.sum(xqh * xqh, axis=-1, keepdims=True) + EPS) * (dh ** -0.5)
        kh = xkh * lax.rsqrt(jnp.sum(xkh * xkh, axis=-1, keepdims=True) + EPS)
        beta_raw = jnp.sum(jnp.where(lane == h, gates, 0.0), axis=-1, keepdims=True)
        alpha_raw = jnp.sum(jnp.where(lane == h + n_heads, gates, 0.0), axis=-1, keepdims=True)
        beta = jnp.broadcast_to(_sigmoid(beta_raw), (t, dh))
        g = jnp.broadcast_to(alpha_raw, (t, dh))
        g = -jnp.exp(alog_ref[hh]) * _softplus(g + dtb_ref[hh])
        for ci in range(n_ch):
            sl = slice(ci * c, (ci + 1) * c)
            gc = g[sl]
            d = 1
            while d < c:
                gc = gc + _shift_rows(gc, d, 0.0, rows)
                d *= 2
            gct = jnp.transpose(gc)
            decay = jnp.where(lower, jnp.exp(jnp.where(lower, gc - gct, 0.0)), 0.0)
            kc, bc = kh[sl], beta[sl]
            kb = kc * bc
            pairs.append(dict(hh=hh, ci=ci, q=qh[sl], k=kc, kb=kb, k16=kc.astype(BF16),
                              rhs=jnp.concatenate([xv[sl, hs] * bc, kb * jnp.exp(gc)], axis=1),
                              gc=gc, decay=decay, g_last=gc[c - 1:c, :]))

    for pr in pairs:
        pr["lmat"] = jnp.where(strict, _dot_nt(pr["kb"].astype(BF16), pr["k16"]) * pr["decay"], 0.0)
        pr["pk"] = (-jnp.where(blk[0], pr["lmat"], 0.0)).astype(BF16)
        pr["tinv"] = eye + pr["pk"]
    for _ in range(int(math.log2(DN_BASE)) - 1):
        for pr in pairs:
            pr["pk"] = _dot(pr["pk"], pr["pk"]).astype(BF16)
        for pr in pairs:
            pr["tinv"] = pr["tinv"] + _dot(pr["tinv"].astype(BF16), pr["pk"])
    for lvl in range(len(blk) - 1):
        for pr in pairs:
            l_off = jnp.where(blk[lvl + 1], jnp.where(blk[lvl], 0.0, pr["lmat"]), 0.0)
            pr["t16"] = pr["tinv"].astype(BF16)
            pr["tl"] = _dot(pr["t16"], l_off.astype(BF16)).astype(BF16)
        for pr in pairs:
            pr["tinv"] = pr["tinv"] - _dot(pr["tl"], pr["t16"])
    for pr in pairs:
        x = _dot(pr["tinv"].astype(BF16), pr["rhs"].astype(BF16))
        pr["u"], pr["w16"] = x[:, :dh], x[:, dh:].astype(BF16)
        pr["qk16"] = (_dot_nt(pr["q"].astype(BF16), pr["k16"]) * pr["decay"]).astype(BF16)
        pr["q_dec"] = (pr["q"] * jnp.exp(pr["gc"])).astype(BF16)
        k_tail = pr["k"] * jnp.exp(pr["g_last"] - pr["gc"])
        pr["k_tail_t"] = jnp.transpose(k_tail).astype(BF16)
        pr["gl"] = jnp.exp(pr["g_last"])

    sts = [state[hh] for hh in range(hp)]
    outs = [[None] * n_ch for _ in range(hp)]
    for ci in range(n_ch):
        cur = [pr for pr in pairs if pr["ci"] == ci]
        st16 = [sts[pr["hh"]].astype(BF16) for pr in cur]
        v16 = [(pr["u"] - _dot(pr["w16"], s16)).astype(BF16) for pr, s16 in zip(cur, st16)]
        for pr, s16, vn in zip(cur, st16, v16):
            outs[pr["hh"]][ci] = _dot(pr["q_dec"], s16) + _dot(pr["qk16"], vn)
        for pr, vn in zip(cur, v16):
            sts[pr["hh"]] = sts[pr["hh"]] * pr["gl"] + _dot(pr["k_tail_t"], vn)
    for hh in range(hp):
        state[hh] = sts[hh]
        hs = slice(hh * dh, (hh + 1) * dh)
        o = outs[hh][0] if n_ch == 1 else jnp.concatenate(outs[hh], axis=0)
        o = _rms(o, nw_ref[...]) * _silu(z_ref[:, hs])
        o_ref[:, hs] = o.astype(BF16)


def _deltanet_group(p, cols, col_gates, conv_w, a_log, dt_bias, norm_w):
    s = p.shape[0]
    n_heads = a_log.shape[0]
    dh = norm_w.shape[0]
    width = n_heads * dh
    col_q, col_k, col_v, col_z = cols
    hp = max(h_ for h_ in (4, 2, 1)
             if n_heads % h_ == 0 and all(col % (h_ * dh) == 0 for col in cols))
    t = _pick(s, (256, 128))
    lanes = lambda v: jnp.broadcast_to(v.reshape(n_heads, 1, 1), (n_heads, 1, dh))

    def pcol(col):
        return pl.BlockSpec((t, hp * dh), lambda h, i: (i, col // (hp * dh) + h))

    def ccol(grp):
        return pl.BlockSpec((CONV_W, hp * dh), lambda h, i: (0, grp * (n_heads // hp) + h))

    return pl.pallas_call(
        functools.partial(_deltanet_kernel, n_heads=n_heads, hp=hp),
        out_shape=jax.ShapeDtypeStruct((s, width), BF16),
        grid=(n_heads // hp, s // t),
        in_specs=[
            pcol(col_q), pcol(col_k), pcol(col_v), pcol(col_z),
            pl.BlockSpec((t, LANES), lambda h, i: (i, col_gates // LANES)),
            ccol(0), ccol(1), ccol(2),
            pl.BlockSpec((hp, 1, dh), lambda h, i: (h, 0, 0)),
            pl.BlockSpec((hp, 1, dh), lambda h, i: (h, 0, 0)),
            pl.BlockSpec((1, dh), lambda h, i: (0, 0)),
        ],
        out_specs=pl.BlockSpec((t, hp * dh), lambda h, i: (i, h)),
        scratch_shapes=[pltpu.VMEM((t + 8, hp * dh), F32)] * 3 + [pltpu.VMEM((hp, dh, dh), F32)],
        compiler_params=_cparams(("parallel", "arbitrary")),
        name="deltanet_group",
    )(p, p, p, p, p, conv_w, conv_w, conv_w, lanes(a_log), lanes(dt_bias),
      norm_w.reshape(1, dh))


def _mla_proj_kernel(cq_ref, ckv_ref, ckr_ref, pos_ref, inv_ref, qn_ref, kvn_ref,
                     wq_ref, wkv_ref, q_out, k_out, v_out, hq, hkv, cos_t, sin_t, kpe,
                     *, scale):
    @pl.when(pl.program_id(1) == 0)
    def _():
        hq[...] = _rms(cq_ref[...], qn_ref[...]).astype(BF16)
        hkv[...] = _rms(ckv_ref[...], kvn_ref[...]).astype(BF16)
        ang = pos_ref[...].astype(F32) * inv_ref[...]
        lane = lax.broadcasted_iota(jnp.int32, ang.shape, 1)
        cs = jnp.where(lane < C_ROPE, jnp.cos(ang), 0.0)
        sn = jnp.where(lane < C_ROPE, jnp.sin(ang), 0.0)
        cos_t[...] = cs
        sin_t[...] = sn
        r2 = ckr_ref[...]
        kpe[...] = (r2 * cs + pltpu.roll(r2, C_ROPE, axis=1) * sn).astype(BF16)

    rq = _dot(hq[...], wq_ref[...])
    r2 = rq[:, C_NOPE:]
    q_pe = r2 * cos_t[...] + pltpu.roll(r2, C_ROPE, axis=1) * sin_t[...]
    q_out[:, :C_NOPE] = (rq[:, :C_NOPE] * scale).astype(BF16)
    q_out[:, C_NOPE:] = (q_pe * scale).astype(BF16)

    rkv = _dot(hkv[...], wkv_ref[...])
    k_out[:, :C_NOPE] = rkv[:, :C_NOPE].astype(BF16)
    k_out[:, C_NOPE:] = kpe[...]
    tkv = v_out.shape[2]
    for ck in range(v_out.shape[0]):
        v_out[ck] = jnp.transpose(rkv[ck * tkv:(ck + 1) * tkv, C_NOPE:]).astype(BF16)


def _rope_rot_cols(w):
    half = C_ROPE // 2
    return jnp.concatenate([-w[..., half:], w[..., :half]], axis=-1)


def _mla_projections(p, col_q, col_kv, col_kr, positions, q_norm_w, w_q_up, kv_norm_w, w_kv_up):
    s = p.shape[0]
    q_lora, n_heads, dqk = w_q_up.shape
    kv_lora = w_kv_up.shape[0]
    dv = w_kv_up.shape[2] - C_NOPE
    assert dqk == C_NOPE + C_ROPE and 2 * C_ROPE == LANES
    t = _pick(s, (512, 256, 128))
    tkv = min(t, ATTN_TK)
    w_pe = w_q_up[..., C_NOPE:]
    wq = jnp.concatenate([w_q_up[..., :C_NOPE], w_pe, _rope_rot_cols(w_pe)], axis=-1)
    wq = jnp.transpose(wq, (1, 0, 2)).astype(BF16)
    wkv = jnp.transpose(w_kv_up, (1, 0, 2)).astype(BF16)
    inv = ROPE_THETA ** (-jnp.arange(0, C_ROPE, 2, dtype=F32) / C_ROPE)
    inv = jnp.concatenate([inv, inv, jnp.zeros((LANES - C_ROPE,), F32)]).reshape(1, LANES)
    dk = C_NOPE + LANES
    scale = float(dqk) ** -0.5 * math.log2(math.e)
    full = lambda i, h: (0, 0)
    return pl.pallas_call(
        functools.partial(_mla_proj_kernel, scale=scale),
        out_shape=(jax.ShapeDtypeStruct((n_heads, s, dk), BF16),
                   jax.ShapeDtypeStruct((n_heads, s, dk), BF16),
                   jax.ShapeDtypeStruct((n_heads, s // tkv, dv, tkv), BF16)),
        grid=(s // t, n_heads),
        in_specs=[
            pl.BlockSpec((t, q_lora), lambda i, h: (i, col_q // q_lora)),
            pl.BlockSpec((t, kv_lora), lambda i, h: (i, col_kv // kv_lora)),
            pl.BlockSpec((t, LANES), lambda i, h: (i, col_kr // LANES)),
            pl.BlockSpec((t, 1), lambda i, h: (i, 0)),
            pl.BlockSpec((1, LANES), full),
            pl.BlockSpec((1, q_lora), full),
            pl.BlockSpec((1, kv_lora), full),
            pl.BlockSpec((None, q_lora, dk), lambda i, h: (h, 0, 0)),
            pl.BlockSpec((None, kv_lora, C_NOPE + dv), lambda i, h: (h, 0, 0)),
        ],
        out_specs=(pl.BlockSpec((None, t, dk), lambda i, h: (h, i, 0)),
                   pl.BlockSpec((None, t, dk), lambda i, h: (h, i, 0)),
                   pl.BlockSpec((None, t // tkv, dv, tkv), lambda i, h: (h, i, 0, 0))),
        scratch_shapes=[pltpu.VMEM((t, q_lora), BF16), pltpu.VMEM((t, kv_lora), BF16),
                        pltpu.VMEM((t, LANES), F32), pltpu.VMEM((t, LANES), F32),
                        pltpu.VMEM((t, LANES), BF16)],
        compiler_params=_cparams(("parallel", "arbitrary")),
        name="mla_projections",
    )(p, p, p, positions.reshape(s, 1), inv, q_norm_w.reshape(1, q_lora),
      kv_norm_w.reshape(1, kv_lora), wq, wkv)


def _flash_kernel(q_ref, k_ref, vt_ref, o_ref, s_scr):
    i = pl.program_id(1)
    tq = q_ref.shape[0]
    tk, dv = k_ref.shape[1], vt_ref.shape[1]
    q = q_ref[...]

    def scores(j):
        return _dot_nt(k_ref[j], q)

    def update(state, s, j):
        m, l, acc = state
        m_new = jnp.maximum(m, jnp.max(s, axis=0, keepdims=True))
        alpha = jnp.exp2(m - m_new)
        p = jnp.exp2(s - m_new)
        l = alpha * l + jnp.sum(p, axis=0, keepdims=True)
        acc = alpha * acc + _dot(vt_ref[j], p.astype(BF16))
        return m_new, l, acc

    def body(j, state):
        s_next = scores(j + 1)
        state = update(state, s_scr[...], j)
        s_scr[...] = s_next
        return state

    s_scr[...] = scores(0)
    init = (jnp.full((1, tq), NEG_BIG, F32), jnp.zeros((1, tq), F32), jnp.zeros((dv, tq), F32))
    state = lax.fori_loop(0, i, body, init)
    rows = lax.broadcasted_iota(jnp.int32, (tk, tq), 0)
    cols = lax.broadcasted_iota(jnp.int32, (tk, tq), 1)
    _, l, acc = update(state, jnp.where(rows <= cols, s_scr[...], NEG_BIG), i)
    o_ref[...] = jnp.transpose(acc / l).astype(o_ref.dtype)


def _causal_attention(q, k, vt):
    n_heads, s, dk = q.shape
    _, n_chunks, dv, tk = vt.shape
    k = k.reshape(n_heads, n_chunks, tk, dk)
    return pl.pallas_call(
        _flash_kernel,
        out_shape=jax.ShapeDtypeStruct((s, n_heads * dv), BF16),
        grid=(n_heads, n_chunks),
        in_specs=[
            pl.BlockSpec((None, tk, dk), lambda h, i: (h, i, 0)),
            pl.BlockSpec((None, n_chunks, tk, dk), lambda h, i: (h, 0, 0, 0)),
            pl.BlockSpec((None, n_chunks, dv, tk), lambda h, i: (h, 0, 0, 0)),
        ],
        out_specs=pl.BlockSpec((tk, dv), lambda h, i: (i, h)),
        scratch_shapes=[pltpu.VMEM((tk, tk), F32)],
        compiler_params=_cparams(("parallel", "arbitrary")),
        name="mla_flash_attention",
    )(q, k, vt)


def _outproj_kernel(ya_ref, yb_ref, yc_ref, nc_ref, w_ref, x_ref, g_ref, o_ref, ycn):
    wa, wb = ya_ref.shape[1], yb_ref.shape[1]

    @pl.when(pl.program_id(1) == 0)
    def _():
        ycn[...] = _rms(yc_ref[...].astype(F32), nc_ref[...]).astype(BF16)

    acc = _dot(ya_ref[...], w_ref[0:wa, :])
    acc += _dot(yb_ref[...], w_ref[wa:wa + wb, :])
    acc += _dot(ycn[...], w_ref[wa + wb:, :])
    o_ref[...] = x_ref[...] + g_ref[...] * acc


def _out_projection(ya, yb, yc, norm_c, w_bf16, x, gate):
    s, d = x.shape
    wa, wb, wc = ya.shape[1], yb.shape[1], yc.shape[1]
    tm = _pick(s, (512, 256, 128))
    tn = _pick(d, (1024, 512, 256, 128))
    return pl.pallas_call(
        _outproj_kernel,
        out_shape=jax.ShapeDtypeStruct((s, d), F32),
        grid=(s // tm, d // tn),
        in_specs=[
            pl.BlockSpec((tm, wa), lambda i, j: (i, 0)),
            pl.BlockSpec((tm, wb), lambda i, j: (i, 0)),
            pl.BlockSpec((tm, wc), lambda i, j: (i, 0)),
            pl.BlockSpec((1, wc), lambda i, j: (0, 0)),
            pl.BlockSpec((wa + wb + wc, tn), lambda i, j: (0, j)),
            pl.BlockSpec((tm, tn), lambda i, j: (i, j)),
            pl.BlockSpec((1, tn), lambda i, j: (0, j)),
        ],
        out_specs=pl.BlockSpec((tm, tn), lambda i, j: (i, j)),
        scratch_shapes=[pltpu.VMEM((tm, wc), BF16)],
        compiler_params=_cparams(("parallel", "arbitrary")),
        name="out_projection",
    )(ya, yb, yc, norm_c.reshape(1, wc), w_bf16, x, gate)


def _take_top(work, dst_ref, n):
    for kk in range(n):
        m = jnp.max(work, axis=0, keepdims=True)
        dst_ref[kk:kk + 1, :] = m
        work = jnp.where(work == m, -jnp.inf, work)


def _peer_route_kernel(x_ref, nw_ref, sc_ref, sh_ref, wq_ref, keys_ref,
                       h_out, st_out, tau_out, lse_out, h_scr, top1, top2, cand, best):
    @pl.when(pl.program_id(1) == 0)
    def _():
        y = _rms(x_ref[...], nw_ref[...])
        hf = y * (1.0 + sc_ref[...]) + sh_ref[...]
        h_scr[...] = hf.astype(BF16)
        h_out[...] = jnp.transpose(hf).astype(BF16)

    half = keys_ref.shape[2]
    q16 = _dot(h_scr[...], wq_ref[...]).astype(BF16)
    for part, top in ((0, top1), (1, top2)):
        s_t = _dot_nt(keys_ref[part], q16[:, part * half:(part + 1) * half])
        st_out[part] = s_t
        _take_top(s_t, top, PEER_TOPK)

    k = PEER_TOPK
    cand[0:k, :] = top1[0:1, :] + top2[0:k, :]
    for i in range(1, k // 2):
        cand[k + (i - 1) * (k // 2):k + i * (k // 2), :] = top1[i:i + 1, :] + top2[0:k // 2, :]
    base = k + (k // 2 - 1) * (k // 2)
    cand[base:base + k // 2, :] = top1[k // 2:k, :] + top2[0:1, :]
    _take_top(cand[...], best, k)

    b = best[...]
    m0 = b[0:1, :]
    z = jnp.sum(jnp.exp(b - m0), axis=0, keepdims=True)
    tau_out[...] = b[k - 1:k, :]
    lse_out[...] = m0 + jnp.log(z)


def _peer_route(x, norm_w, sc, sh, wq_bf16, keys_bf16):
    s, d = x.shape
    n_heads, _, n_keys, half = keys_bf16.shape
    assert PEER_TOPK == 16 and n_keys % 8 == 0
    tm = _pick(s, (512, 256, 128))
    n_cand = PEER_TOPK + (PEER_TOPK // 2) * (PEER_TOPK // 2)
    row = lambda i, h: (0, 0)
    return pl.pallas_call(
        _peer_route_kernel,
        out_shape=(jax.ShapeDtypeStruct((d, s), BF16),
                   jax.ShapeDtypeStruct((n_heads, 2, n_keys, s), F32),
                   jax.ShapeDtypeStruct((n_heads, 1, s), F32),
                   jax.ShapeDtypeStruct((n_heads, 1, s), F32)),
        grid=(s // tm, n_heads),
        in_specs=[
            pl.BlockSpec((tm, d), lambda i, h: (i, 0)),
            pl.BlockSpec((1, d), row),
            pl.BlockSpec((1, d), row),
            pl.BlockSpec((1, d), row),
            pl.BlockSpec((d, 2 * half), lambda i, h: (0, h)),
            pl.BlockSpec((None, 2, n_keys, half), lambda i, h: (h, 0, 0, 0)),
        ],
        out_specs=(pl.BlockSpec((d, tm), lambda i, h: (0, i)),
                   pl.BlockSpec((None, 2, n_keys, tm), lambda i, h: (h, 0, 0, i)),
                   pl.BlockSpec((None, 1, tm), lambda i, h: (h, 0, i)),
                   pl.BlockSpec((None, 1, tm), lambda i, h: (h, 0, i))),
        scratch_shapes=[pltpu.VMEM((tm, d), BF16),
                        pltpu.VMEM((PEER_TOPK, tm), F32), pltpu.VMEM((PEER_TOPK, tm), F32),
                        pltpu.VMEM((n_cand, tm), F32), pltpu.VMEM((PEER_TOPK, tm), F32)],
        compiler_params=_cparams(("parallel", "arbitrary")),
        name="peer_route",
    )(x, norm_w, sc, sh, wq_bf16, keys_bf16)


def _peer_dense_kernel(ht_ref, u_ref, vt_ref, st_ref, tau_ref, lse_ref, acc_ref, w_scr):
    j = pl.program_id(1)
    n_heads, _, n_keys, tm = st_ref.shape
    tn = u_ref.shape[0]

    @pl.when(j == 0)
    def _():
        acc_ref[...] = jnp.zeros_like(acc_ref)

    raw = _dot(u_ref[...], ht_ref[...])

    for ab in range(tn // n_keys):
        a = j * (tn // n_keys) + ab
        s1_rows = [st_ref[hd, 0, pl.ds(a, 1), :] for hd in range(n_heads)]
        for ls in range(tm // LANES):
            lanes = slice(ls * LANES, (ls + 1) * LANES)
            w = jnp.zeros((n_keys, LANES), F32)
            for hd in range(n_heads):
                x = st_ref[hd, 1, :, lanes] + s1_rows[hd][:, lanes]
                gate = jnp.exp(x - lse_ref[hd:hd + 1, lanes])
                w = w + jnp.where(x >= tau_ref[hd:hd + 1, lanes], gate, 0.0)
            w_scr[ab * n_keys:(ab + 1) * n_keys, lanes] = w

    act = _gelu_tanh(raw)
    coef = (w_scr[...] * act).astype(BF16)
    acc_ref[...] += _dot(vt_ref[...], coef)


def _peer_dense(ht, u_bf16, vt_bf16, st, tau, lse):
    d, s = ht.shape
    n_exp = u_bf16.shape[0]
    n_heads, _, n_keys, _ = st.shape
    tm = _pick(s, (512, 256, 128))
    tn = _pick(n_exp, (512, 256, 128))
    assert tn % n_keys == 0
    return pl.pallas_call(
        _peer_dense_kernel,
        out_shape=jax.ShapeDtypeStruct((d, s), F32),
        grid=(s // tm, n_exp // tn),
        in_specs=[
            pl.BlockSpec((d, tm), lambda i, j: (0, i), pipeline_mode=pl.Buffered(1)),
            pl.BlockSpec((tn, d), lambda i, j: (j, 0)),
            pl.BlockSpec((d, tn), lambda i, j: (0, j)),
            pl.BlockSpec((n_heads, 2, n_keys, tm), lambda i, j: (0, 0, 0, i),
                         pipeline_mode=pl.Buffered(1)),
            pl.BlockSpec((n_heads, tm), lambda i, j: (0, i)),
            pl.BlockSpec((n_heads, tm), lambda i, j: (0, i)),
        ],
        out_specs=pl.BlockSpec((d, tm), lambda i, j: (0, i)),
        scratch_shapes=[pltpu.VMEM((tn, tm), F32)],
        compiler_params=_cparams(("parallel", "arbitrary")),
        name="peer_dense",
    )(ht, u_bf16, vt_bf16, st, tau, lse)


def _peer_residual_kernel(acc_ref, x_ref, g_ref, fw_ref, o_ref, *, final_norm):
    y = x_ref[...] + g_ref[...] * jnp.transpose(acc_ref[...])
    if final_norm:
        y = _rms(y, fw_ref[...])
    o_ref[...] = y


def _peer_residual(acc_t, x, gate, final_w, final_norm):
    s, d = x.shape
    tm = _pick(s, (256, 128))
    row = lambda i: (0, 0)
    return pl.pallas_call(
        functools.partial(_peer_residual_kernel, final_norm=final_norm),
        out_shape=jax.ShapeDtypeStruct((s, d), F32),
        grid=(s // tm,),
        in_specs=[
            pl.BlockSpec((d, tm), lambda i: (0, i)),
            pl.BlockSpec((tm, d), lambda i: (i, 0)),
            pl.BlockSpec((1, d), row),
            pl.BlockSpec((1, d), row),
        ],
        out_specs=pl.BlockSpec((tm, d), lambda i: (i, 0)),
        compiler_params=_cparams(("parallel",)),
        name="peer_residual",
    )(acc_t, x, gate, final_w.reshape(1, d))


def _pack_in_weights(w_in, a_width, b_width, b_heads, q_lora, kv_lora):
    o_beta = 2 * a_width + 4 * b_width
    o_alpha = o_beta + b_heads
    o_q = o_alpha + b_heads
    o_kv = o_q + q_lora
    o_kr = o_kv + kv_lora
    d = w_in.shape[0]
    w_kr = w_in[:, o_kr:o_kr + C_ROPE]
    n_gate_pad = LANES - 2 * b_heads
    parts = [w_in[:, :o_beta], w_in[:, o_q:o_kv], w_in[:, o_kv:o_kr], w_kr, _rope_rot_cols(w_kr),
             w_in[:, o_beta:o_q], jnp.zeros((d, n_gate_pad), w_in.dtype)]
    cols = dict(rec=0, gate=a_width, q=2 * a_width, k=2 * a_width + b_width,
                v=2 * a_width + 2 * b_width, z=2 * a_width + 3 * b_width,
                cq=o_beta, ckv=o_beta + q_lora, ckr=o_beta + q_lora + kv_lora,
                gates=o_beta + q_lora + kv_lora + LANES)
    n = cols["gates"] + LANES
    n_pad = -n % 768 if n >= 768 else 0
    if n_pad:
        parts.append(jnp.zeros((d, n_pad), w_in.dtype))
    return jnp.concatenate(parts, axis=1).astype(BF16), cols


def kernel(x, c, positions, mod_w, mod_layer, norm_mix_w, w_in, lru_conv_w, lru_conv_b, lru_wa, lru_ba, lru_wx, lru_bx, lru_lambda, dn_conv_w, dn_a_log, dn_dt_bias, dn_norm_w, mla_q_norm_w, mla_w_q_up, mla_kv_norm_w, mla_w_kv_up, branch_norm_a, branch_norm_c, w_out, norm_ffn_w, peer_w_query, peer_sub_keys, peer_u, peer_v, final_norm_w):
    bsz, s, d = x.shape
    assert bsz == 1, "kernels are written for a single sequence"
    depth = w_in.shape[0]
    a_width = lru_conv_w.shape[2]
    b_width = dn_conv_w.shape[2] // 3
    b_heads = dn_a_log.shape[1]
    q_lora = mla_q_norm_w.shape[1]
    kv_lora = mla_kv_norm_w.shape[1]
    p_heads, q_dim = peer_w_query.shape[2], peer_w_query.shape[3]

    mod = _modulation(c, mod_w, mod_layer).reshape(depth, N_MOD, 1, d)
    xs = x.reshape(s, d)
    pos = positions.reshape(s)

    for l in range(depth):
        sh_a, sc_a, g_a, sh_f, sc_f, g_f = [mod[l, i] for i in range(N_MOD)]
        w_in_p, cols = _pack_in_weights(w_in[l], a_width, b_width, b_heads, q_lora, kv_lora)
        p = _in_projection(xs, norm_mix_w[l].reshape(1, d), sc_a, sh_a, w_in_p)

        ya = _rglru_group(p, cols["rec"], cols["gate"], lru_conv_w[l], lru_conv_b[l], lru_wa[l],
                          lru_ba[l], lru_wx[l], lru_bx[l], lru_lambda[l], branch_norm_a[l])
        yb = _deltanet_group(p, (cols["q"], cols["k"], cols["v"], cols["z"]), cols["gates"],
                             dn_conv_w[l], dn_a_log[l], dn_dt_bias[l], dn_norm_w[l])
        q, k, v = _mla_projections(p, cols["cq"], cols["ckv"], cols["ckr"], pos, mla_q_norm_w[l],
                                   mla_w_q_up[l], mla_kv_norm_w[l], mla_w_kv_up[l])
        yc = _causal_attention(q, k, v)
        xs = _out_projection(ya, yb, yc, branch_norm_c[l], w_out[l].astype(BF16), xs, g_a)

        wq = peer_w_query[l].reshape(d, p_heads * q_dim).astype(BF16)
        h2, st, tau, lse = _peer_route(xs, norm_ffn_w[l].reshape(1, d), sc_f, sh_f, wq,
                                       peer_sub_keys[l].astype(BF16))
        acc_t = _peer_dense(h2, peer_u[l].astype(BF16), jnp.transpose(peer_v[l]).astype(BF16), st,
                            tau.reshape(p_heads, s), lse.reshape(p_heads, s))
        xs = _peer_residual(acc_t, xs, g_f, final_norm_w, final_norm=(l == depth - 1))
    return xs.reshape(bsz, s, d)
```

```python
import functools
import math

import jax
import jax.numpy as jnp
from jax import lax
from jax.experimental import pallas as pl
from jax.experimental.pallas import tpu as pltpu

F32 = jnp.float32
BF16 = jnp.bfloat16
EPS = 1e-6
N_MOD = 6
LRU_C = 8.0
CONV_W = 4
C_NOPE = 128
C_ROPE = 64
ROPE_THETA = 10000.0
PEER_TOPK = 16
ATTN_TK = 512
ATTN_UNROLL = 4
DN_CHUNK = 128
DN_BASE = 16
LANES = 128
MXU_WIDTH = 256
NEG_BIG = -1e30

_VMEM_LIMIT = 56 * 1024 * 1024


def _cparams(sem, vmem=_VMEM_LIMIT):
    return pltpu.CompilerParams(dimension_semantics=sem, vmem_limit_bytes=vmem)


def _pick(n, candidates):
    for c in candidates:
        if n % c == 0:
            return c
    raise ValueError(f"no tile in {candidates} divides {n}")


def _sigmoid(x):
    return 1.0 / (1.0 + jnp.exp(-x))


def _softplus(x):
    return jnp.maximum(x, 0.0) + jnp.log1p(jnp.exp(-jnp.abs(x)))


def _silu(x):
    return x * _sigmoid(x)


def _gelu_tanh(x):
    c = math.sqrt(2.0 / math.pi)
    return 0.5 * x * (1.0 + jnp.tanh(c * (x + 0.044715 * (x * x * x))))


def _rms(x, w):
    ms = jnp.mean(x * x, axis=-1, keepdims=True)
    return x * lax.rsqrt(ms + EPS) * w


def _dot(a, b):
    return jnp.dot(a, b, preferred_element_type=F32)


def _dot_nt(a, b):
    return lax.dot_general(a, b, (((1,), (1,)), ((), ())), preferred_element_type=F32)


def _mod_kernel(c_ref, w_ref, ml_ref, o_ref, acc_ref):
    k = pl.program_id(1)

    @pl.when(k == 0)
    def _():
        acc_ref[...] = jnp.zeros_like(acc_ref)

    c = c_ref[...]
    sc = jnp.broadcast_to(_silu(c), (8, c.shape[1]))
    acc_ref[...] += _dot(sc, w_ref[...])

    @pl.when(k == pl.num_programs(1) - 1)
    def _():
        o_ref[...] = acc_ref[0:1, :] + ml_ref[...]


def _modulation(c, mod_w, mod_layer):
    d, n = mod_w.shape
    nl = mod_layer.shape[0]
    tk = _pick(d, (1024, 512, 256, 128))
    tn = _pick(n, (2048, 1024, 512, 256, 128))
    ml = mod_layer.reshape(nl, n)
    return pl.pallas_call(
        _mod_kernel,
        out_shape=jax.ShapeDtypeStruct((nl, n), F32),
        grid=(n // tn, d // tk),
        in_specs=[
            pl.BlockSpec((1, tk), lambda j, k: (0, k)),
            pl.BlockSpec((tk, tn), lambda j, k: (k, j)),
            pl.BlockSpec((nl, tn), lambda j, k: (0, j)),
        ],
        out_specs=pl.BlockSpec((nl, tn), lambda j, k: (0, j)),
        scratch_shapes=[pltpu.VMEM((8, tn), F32)],
        compiler_params=_cparams(("parallel", "arbitrary")),
        name="adaln_modulation",
    )(c, mod_w, ml)


def _inproj_kernel(x_ref, nw_ref, sc_ref, sh_ref, w_ref, o_ref, h_ref):
    @pl.when(pl.program_id(1) == 0)
    def _():
        y = _rms(x_ref[...], nw_ref[...])
        h_ref[...] = (y * (1.0 + sc_ref[...]) + sh_ref[...]).astype(BF16)

    o_ref[...] = _dot(h_ref[...], w_ref[...])


def _in_projection(x, norm_w, sc, sh, w_bf16):
    s, d = x.shape
    n = w_bf16.shape[1]
    tm = _pick(s, (512, 256, 128))
    tn = _pick(n, (768, 512, 256, 128))
    row = lambda i, j: (0, 0)
    return pl.pallas_call(
        _inproj_kernel,
        out_shape=jax.ShapeDtypeStruct((s, n), F32),
        grid=(s // tm, n // tn),
        in_specs=[
            pl.BlockSpec((tm, d), lambda i, j: (i, 0)),
            pl.BlockSpec((1, d), row),
            pl.BlockSpec((1, d), row),
            pl.BlockSpec((1, d), row),
            pl.BlockSpec((d, tn), lambda i, j: (0, j)),
        ],
        out_specs=pl.BlockSpec((tm, tn), lambda i, j: (i, j)),
        scratch_shapes=[pltpu.VMEM((tm, d), BF16)],
        compiler_params=_cparams(("parallel", "arbitrary")),
        name="in_projection",
    )(x, norm_w, sc, sh, w_bf16)


def _causal_conv(buf_ref, x, cw_ref):
    t = x.shape[0]
    buf_ref[8:t + 8, :] = x
    y = cw_ref[CONV_W - 1:CONV_W, :] * x
    for j in range(CONV_W - 1):
        off = 8 - (CONV_W - 1) + j
        y = y + cw_ref[j:j + 1, :] * buf_ref[off:off + t, :]
    buf_ref[0:8, :] = buf_ref[t:t + 8, :]
    return y


def _shift_rows(x, d, fill, row_ids):
    return jnp.where(row_ids >= d, pltpu.roll(x, d, axis=0), fill)


def _rglru_kernel(xr_ref, xg_ref, cw_ref, cb_ref, wa_ref, ba_ref, wx_ref, bx_ref,
                  lam_ref, nw_ref, o_ref, xbuf, hprev, *, heads, head_dim):
    t = xr_ref.shape[0]

    @pl.when(pl.program_id(0) == 0)
    def _():
        xbuf[0:8, :] = jnp.zeros((8, xbuf.shape[1]), F32)
        hprev[...] = jnp.zeros_like(hprev)

    u = _causal_conv(xbuf, xr_ref[...], cw_ref) + cb_ref[...]
    ub = u.astype(BF16)
    ra, rx = [], []
    for h in range(heads):
        uh = ub[:, h * head_dim:(h + 1) * head_dim]
        ra.append(_dot(uh, wa_ref[h]))
        rx.append(_dot(uh, wx_ref[h]))
    r = _sigmoid(jnp.concatenate(ra, axis=1) + ba_ref[...])
    gate_i = _sigmoid(jnp.concatenate(rx, axis=1) + bx_ref[...])
    log_a = (-LRU_C) * r * _softplus(-lam_ref[...])
    a = jnp.exp(log_a)
    one_minus_a2 = jnp.tanh(-log_a) * (a * a + 1.0)
    b = jnp.sqrt(jnp.maximum(one_minus_a2, 0.0)) * (gate_i * u)

    row_ids = lax.broadcasted_iota(jnp.int32, a.shape, 0)
    d = 1
    while d < t:
        a_sh = _shift_rows(a, d, 1.0, row_ids)
        b_sh = _shift_rows(b, d, 0.0, row_ids)
        b = a * b_sh + b
        a = a * a_sh
        d *= 2
    hs = b + a * hprev[...]
    hprev[...] = hs[t - 1:t, :]

    y = _gelu_tanh(xg_ref[...]) * hs
    o_ref[...] = _rms(y, nw_ref[...]).astype(BF16)


def _rglru_group(p, col_rec, col_gate, conv_w, conv_b, wa, ba, wx, bx, lam, norm_w):
    s = p.shape[0]
    heads, head_dim = wa.shape[0], wa.shape[1]
    width = heads * head_dim
    t = _pick(s, (256, 128))
    vec = lambda v: v.reshape(1, width)
    full = lambda i: (0, 0)
    return pl.pallas_call(
        functools.partial(_rglru_kernel, heads=heads, head_dim=head_dim),
        out_shape=jax.ShapeDtypeStruct((s, width), BF16),
        grid=(s // t,),
        in_specs=[
            pl.BlockSpec((t, width), lambda i: (i, col_rec // width)),
            pl.BlockSpec((t, width), lambda i: (i, col_gate // width)),
            pl.BlockSpec((CONV_W, width), full),
            pl.BlockSpec((1, width), full),
            pl.BlockSpec((heads, head_dim, head_dim), lambda i: (0, 0, 0)),
            pl.BlockSpec((1, width), full),
            pl.BlockSpec((heads, head_dim, head_dim), lambda i: (0, 0, 0)),
            pl.BlockSpec((1, width), full),
            pl.BlockSpec((1, width), full),
            pl.BlockSpec((1, width), full),
        ],
        out_specs=pl.BlockSpec((t, width), lambda i: (i, 0)),
        scratch_shapes=[pltpu.VMEM((t + 8, width), F32), pltpu.VMEM((1, width), F32)],
        compiler_params=_cparams(("arbitrary",)),
        name="rglru_group",
    )(p, p, conv_w, vec(conv_b), wa.astype(BF16), vec(ba), wx.astype(BF16), vec(bx),
      vec(lam), vec(norm_w))


def _deltanet_kernel(q_ref, k_ref, v_ref, z_ref, g_ref, cwq_ref, cwk_ref, cwv_ref,
                     alog_ref, dtb_ref, nw_ref, o_ref, qbuf, kbuf, vbuf, state,
                     *, n_heads, hp):
    hg = pl.program_id(0)
    t = q_ref.shape[0]
    dh = q_ref.shape[1] // hp
    c = DN_CHUNK
    n_ch = t // c
    assert dh == c, "decay matrix is built from a (chunk, head_dim) lane-replicated tile"

    @pl.when(pl.program_id(1) == 0)
    def _():
        z8 = jnp.zeros((8, hp * dh), F32)
        qbuf[0:8, :] = z8
        kbuf[0:8, :] = z8
        vbuf[0:8, :] = z8
        state[...] = jnp.zeros_like(state)

    xq = _silu(_causal_conv(qbuf, q_ref[...], cwq_ref))
    xk = _silu(_causal_conv(kbuf, k_ref[...], cwk_ref))
    xv = _silu(_causal_conv(vbuf, v_ref[...], cwv_ref))
    gates = g_ref[...]
    lane = lax.broadcasted_iota(jnp.int32, gates.shape, 1)

    rows = lax.broadcasted_iota(jnp.int32, (c, c), 0)
    cols = lax.broadcasted_iota(jnp.int32, (c, c), 1)
    lower = rows >= cols
    strict = rows > cols
    eye = jnp.where(rows == cols, 1.0, 0.0)
    blk = []
    b = DN_BASE
    while b <= c:
        sh = int(math.log2(b))
        blk.append((rows >> sh) == (cols >> sh))
        b *= 2

    pairs = []
    for hh in range(hp):
        h = hg * hp + hh
        hs = slice(hh * dh, (hh + 1) * dh)
        xqh, xkh = xq[:, hs], xk[:, hs]
        qh = xqh * lax.rsqrt(jnp.sum(xqh * xqh, axis=-1, keepdims=True) + EPS) * (dh ** -0.5)
        kh = xkh * lax.rsqrt(jnp.sum(xkh * xkh, axis=-1, keepdims=True) + EPS)
        beta_raw = jnp.sum(jnp.where(lane == h, gates, 0.0), axis=-1, keepdims=True)
        alpha_raw = jnp.sum(jnp.where(lane == h + n_heads, gates, 0.0), axis=-1, keepdims=True)
        beta = jnp.broadcast_to(_sigmoid(beta_raw), (t, dh))
        g = jnp.broadcast_to(alpha_raw, (t, dh))
        g = -jnp.exp(alog_ref[hh]) * _softplus(g + dtb_ref[hh])
        for ci in range(n_ch):
            sl = slice(ci * c, (ci + 1) * c)
            gc = g[sl]
            d = 1
            while d < c:
                gc = gc + _shift_rows(gc, d, 0.0, rows)
                d *= 2
            gct = jnp.transpose(gc)
            decay = jnp.where(lower, jnp.exp(jnp.where(lower, gc - gct, 0.0)), 0.0)
            kc, bc = kh[sl], beta[sl]
            kb = kc * bc
            pairs.append(dict(hh=hh, ci=ci, q=qh[sl], k=kc, kb=kb, k16=kc.astype(BF16),
                              rhs=jnp.concatenate([xv[sl, hs] * bc, kb * jnp.exp(gc)], axis=1),
                              gc=gc, decay=decay, g_last=gc[c - 1:c, :]))

    for pr in pairs:
        pr["lmat"] = jnp.where(strict, _dot_nt(pr["kb"].astype(BF16), pr["k16"]) * pr["decay"], 0.0)
        pr["pk"] = (-jnp.where(blk[0], pr["lmat"], 0.0)).astype(BF16)
        pr["tinv"] = eye + pr["pk"]
    for _ in range(int(math.log2(DN_BASE)) - 1):
        for pr in pairs:
            pr["pk"] = _dot(pr["pk"], pr["pk"]).astype(BF16)
        for pr in pairs:
            pr["tinv"] = pr["tinv"] + _dot(pr["tinv"].astype(BF16), pr["pk"])
    for lvl in range(len(blk) - 1):
        for pr in pairs:
            l_off = jnp.where(blk[lvl + 1], jnp.where(blk[lvl], 0.0, pr["lmat"]), 0.0)
            pr["t16"] = pr["tinv"].astype(BF16)
            pr["tl"] = _dot(pr["t16"], l_off.astype(BF16)).astype(BF16)
        for pr in pairs:
            pr["tinv"] = pr["tinv"] - _dot(pr["tl"], pr["t16"])
    for pr in pairs:
        x = _dot(pr["tinv"].astype(BF16), pr["rhs"].astype(BF16))
        pr["u"], pr["w16"] = x[:, :dh], x[:, dh:].astype(BF16)
        pr["qk16"] = (_dot_nt(pr["q"].astype(BF16), pr["k16"]) * pr["decay"]).astype(BF16)
        pr["q_dec"] = (pr["q"] * jnp.exp(pr["gc"])).astype(BF16)
        k_tail = pr["k"] * jnp.exp(pr["g_last"] - pr["gc"])
        pr["k_tail_t"] = jnp.transpose(k_tail).astype(BF16)
        pr["gl"] = jnp.exp(pr["g_last"])

    sts = [state[hh] for hh in range(hp)]
    outs = [[None] * n_ch for _ in range(hp)]
    for ci in range(n_ch):
        cur = [pr for pr in pairs if pr["ci"] == ci]
        st16 = [sts[pr["hh"]].astype(BF16) for pr in cur]
        v16 = [(pr["u"] - _dot(pr["w16"], s16)).astype(BF16) for pr, s16 in zip(cur, st16)]
        for pr, s16, vn in zip(cur, st16, v16):
            outs[pr["hh"]][ci] = _dot(pr["q_dec"], s16) + _dot(pr["qk16"], vn)
        for pr, vn in zip(cur, v16):
            sts[pr["hh"]] = sts[pr["hh"]] * pr["gl"] + _dot(pr["k_tail_t"], vn)
    for hh in range(hp):
        state[hh] = sts[hh]
        hs = slice(hh * dh, (hh + 1) * dh)
        o = outs[hh][0] if n_ch == 1 else jnp.concatenate(outs[hh], axis=0)
        o = _rms(o, nw_ref[...]) * _silu(z_ref[:, hs])
        o_ref[:, hs] = o.astype(BF16)


def _deltanet_group(p, cols, col_gates, conv_w, a_log, dt_bias, norm_w):
    s = p.shape[0]
    n_heads = a_log.shape[0]
    dh = norm_w.shape[0]
    width = n_heads * dh
    col_q, col_k, col_v, col_z = cols
    hp = max(h_ for h_ in (4, 2, 1)
             if n_heads % h_ == 0 and all(col % (h_ * dh) == 0 for col in cols))
    t = _pick(s, (256, 128))
    lanes = lambda v: jnp.broadcast_to(v.reshape(n_heads, 1, 1), (n_heads, 1, dh))

    def pcol(col):
        return pl.BlockSpec((t, hp * dh), lambda h, i: (i, col // (hp * dh) + h))

    def ccol(grp):
        return pl.BlockSpec((CONV_W, hp * dh), lambda h, i: (0, grp * (n_heads // hp) + h))

    return pl.pallas_call(
        functools.partial(_deltanet_kernel, n_heads=n_heads, hp=hp),
        out_shape=jax.ShapeDtypeStruct((s, width), BF16),
        grid=(n_heads // hp, s // t),
        in_specs=[
            pcol(col_q), pcol(col_k), pcol(col_v), pcol(col_z),
            pl.BlockSpec((t, LANES), lambda h, i: (i, col_gates // LANES)),
            ccol(0), ccol(1), ccol(2),
            pl.BlockSpec((hp, 1, dh), lambda h, i: (h, 0, 0)),
            pl.BlockSpec((hp, 1, dh), lambda h, i: (h, 0, 0)),
            pl.BlockSpec((1, dh), lambda h, i: (0, 0)),
        ],
        out_specs=pl.BlockSpec((t, hp * dh), lambda h, i: (i, h)),
        scratch_shapes=[pltpu.VMEM((t + 8, hp * dh), F32)] * 3 + [pltpu.VMEM((hp, dh, dh), F32)],
        compiler_params=_cparams(("parallel", "arbitrary")),
        name="deltanet_group",
    )(p, p, p, p, p, conv_w, conv_w, conv_w, lanes(a_log), lanes(dt_bias),
      norm_w.reshape(1, dh))


def _mla_proj_kernel(cq_ref, ckv_ref, ckr_ref, pos_ref, inv_ref, qn_ref, kvn_ref,
                     wq_ref, wkv_ref, q_out, k_out, v_out, hq, hkv, cos_t, sin_t, kpe,
                     *, scale):
    @pl.when(pl.program_id(1) == 0)
    def _():
        hq[...] = _rms(cq_ref[...], qn_ref[...]).astype(BF16)
        hkv[...] = _rms(ckv_ref[...], kvn_ref[...]).astype(BF16)
        ang = pos_ref[...].astype(F32) * inv_ref[...]
        lane = lax.broadcasted_iota(jnp.int32, ang.shape, 1)
        cs = jnp.where(lane < C_ROPE, jnp.cos(ang), 0.0)
        sn = jnp.where(lane < C_ROPE, jnp.sin(ang), 0.0)
        cos_t[...] = cs
        sin_t[...] = sn
        r2 = ckr_ref[...]
        kpe[...] = (r2 * cs + pltpu.roll(r2, C_ROPE, axis=1) * sn).astype(BF16)

    rq = _dot(hq[...], wq_ref[...])
    r2 = rq[:, C_NOPE:]
    q_pe = r2 * cos_t[...] + pltpu.roll(r2, C_ROPE, axis=1) * sin_t[...]
    q_out[:, :C_NOPE] = (rq[:, :C_NOPE] * scale).astype(BF16)
    q_out[:, C_NOPE:] = (q_pe * scale).astype(BF16)

    rkv = _dot(hkv[...], wkv_ref[...])
    k_out[:, :C_NOPE] = rkv[:, :C_NOPE].astype(BF16)
    k_out[:, C_NOPE:] = kpe[...]
    tkv = v_out.shape[2]
    for ck in range(v_out.shape[0]):
        v_out[ck] = jnp.transpose(rkv[ck * tkv:(ck + 1) * tkv, C_NOPE:]).astype(BF16)


def _rope_rot_cols(w):
    half = C_ROPE // 2
    return jnp.concatenate([-w[..., half:], w[..., :half]], axis=-1)


def _mla_projections(p, col_q, col_kv, col_kr, positions, q_norm_w, w_q_up, kv_norm_w, w_kv_up):
    s = p.shape[0]
    q_lora, n_heads, dqk = w_q_up.shape
    kv_lora = w_kv_up.shape[0]
    dv = w_kv_up.shape[2] - C_NOPE
    assert dqk == C_NOPE + C_ROPE and 2 * C_ROPE == LANES
    t = _pick(s, (512, 256, 128))
    tkv = min(t, ATTN_TK)
    w_pe = w_q_up[..., C_NOPE:]
    wq = jnp.concatenate([w_q_up[..., :C_NOPE], w_pe, _rope_rot_cols(w_pe)], axis=-1)
    wq = jnp.transpose(wq, (1, 0, 2)).astype(BF16)
    wkv = jnp.transpose(w_kv_up, (1, 0, 2)).astype(BF16)
    inv = ROPE_THETA ** (-jnp.arange(0, C_ROPE, 2, dtype=F32) / C_ROPE)
    inv = jnp.concatenate([inv, inv, jnp.zeros((LANES - C_ROPE,), F32)]).reshape(1, LANES)
    dk = C_NOPE + LANES
    scale = float(dqk) ** -0.5 * math.log2(math.e)
    full = lambda i, h: (0, 0)
    return pl.pallas_call(
        functools.partial(_mla_proj_kernel, scale=scale),
        out_shape=(jax.ShapeDtypeStruct((n_heads, s, dk), BF16),
                   jax.ShapeDtypeStruct((n_heads, s, dk), BF16),
                   jax.ShapeDtypeStruct((n_heads, s // tkv, dv, tkv), BF16)),
        grid=(s // t, n_heads),
        in_specs=[
            pl.BlockSpec((t, q_lora), lambda i, h: (i, col_q // q_lora)),
            pl.BlockSpec((t, kv_lora), lambda i, h: (i, col_kv // kv_lora)),
            pl.BlockSpec((t, LANES), lambda i, h: (i, col_kr // LANES)),
            pl.BlockSpec((t, 1), lambda i, h: (i, 0)),
            pl.BlockSpec((1, LANES), full),
            pl.BlockSpec((1, q_lora), full),
            pl.BlockSpec((1, kv_lora), full),
            pl.BlockSpec((None, q_lora, dk), lambda i, h: (h, 0, 0)),
            pl.BlockSpec((None, kv_lora, C_NOPE + dv), lambda i, h: (h, 0, 0)),
        ],
        out_specs=(pl.BlockSpec((None, t, dk), lambda i, h: (h, i, 0)),
                   pl.BlockSpec((None, t, dk), lambda i, h: (h, i, 0)),
                   pl.BlockSpec((None, t // tkv, dv, tkv), lambda i, h: (h, i, 0, 0))),
        scratch_shapes=[pltpu.VMEM((t, q_lora), BF16), pltpu.VMEM((t, kv_lora), BF16),
                        pltpu.VMEM((t, LANES), F32), pltpu.VMEM((t, LANES), F32),
                        pltpu.VMEM((t, LANES), BF16)],
        compiler_params=_cparams(("parallel", "arbitrary")),
        name="mla_projections",
    )(p, p, p, positions.reshape(s, 1), inv, q_norm_w.reshape(1, q_lora),
      kv_norm_w.reshape(1, kv_lora), wq, wkv)


def _flash_kernel(q_ref, k_ref, vt_ref, o_ref, s_scr):
    i = pl.program_id(1)
    tq = q_ref.shape[0]
    tk, dv = k_ref.shape[1], vt_ref.shape[1]
    q = q_ref[...]

    def scores(j):
        return _dot_nt(k_ref[j], q)

    def update(state, s, j):
        m, l, acc = state
        m_new = jnp.maximum(m, jnp.max(s, axis=0, keepdims=True))
        alpha = jnp.exp2(m - m_new)
        p = jnp.exp2(s - m_new)
        l = alpha * l + jnp.sum(p, axis=0, keepdims=True)
        acc = alpha * acc + _dot(vt_ref[j], p.astype(BF16))
        return m_new, l, acc

    def body(j, state):
        s_next = scores(j + 1)
        state = update(state, s_scr[...], j)
        s_scr[...] = s_next
        return state

    def body_unrolled(g, state):
        for r in range(ATTN_UNROLL):
            state = body(g * ATTN_UNROLL + r, state)
        return state

    s_scr[...] = scores(0)
    init = (jnp.full((1, tq), NEG_BIG, F32), jnp.zeros((1, tq), F32), jnp.zeros((dv, tq), F32))
    n_main = lax.shift_right_logical(i, int(math.log2(ATTN_UNROLL)))
    state = lax.fori_loop(0, n_main, body_unrolled, init)
    state = lax.fori_loop(n_main * ATTN_UNROLL, i, body, state)
    rows = lax.broadcasted_iota(jnp.int32, (tk, tq), 0)
    cols = lax.broadcasted_iota(jnp.int32, (tk, tq), 1)
    _, l, acc = update(state, jnp.where(rows <= cols, s_scr[...], NEG_BIG), i)
    o_ref[...] = jnp.transpose(acc / l).astype(o_ref.dtype)


def _causal_attention(q, k, vt):
    n_heads, s, dk = q.shape
    _, n_chunks, dv, tk = vt.shape
    k = k.reshape(n_heads, n_chunks, tk, dk)
    return pl.pallas_call(
        _flash_kernel,
        out_shape=jax.ShapeDtypeStruct((s, n_heads * dv), BF16),
        grid=(n_heads, n_chunks),
        in_specs=[
            pl.BlockSpec((None, tk, dk), lambda h, i: (h, i, 0)),
            pl.BlockSpec((None, n_chunks, tk, dk), lambda h, i: (h, 0, 0, 0)),
            pl.BlockSpec((None, n_chunks, dv, tk), lambda h, i: (h, 0, 0, 0)),
        ],
        out_specs=pl.BlockSpec((tk, dv), lambda h, i: (i, h)),
        scratch_shapes=[pltpu.VMEM((tk, tk), F32)],
        compiler_params=_cparams(("parallel", "arbitrary")),
        name="mla_flash_attention",
    )(q, k, vt)


def _outproj_kernel(ya_ref, yb_ref, yc_ref, nc_ref, w_ref, x_ref, g_ref, o_ref, ycn):
    wa, wb = ya_ref.shape[1], yb_ref.shape[1]

    @pl.when(pl.program_id(1) == 0)
    def _():
        ycn[...] = _rms(yc_ref[...].astype(F32), nc_ref[...]).astype(BF16)

    acc = _dot(ya_ref[...], w_ref[0:wa, :])
    acc += _dot(yb_ref[...], w_ref[wa:wa + wb, :])
    acc += _dot(ycn[...], w_ref[wa + wb:, :])
    o_ref[...] = x_ref[...] + g_ref[...] * acc


def _out_projection(ya, yb, yc, norm_c, w_bf16, x, gate):
    s, d = x.shape
    wa, wb, wc = ya.shape[1], yb.shape[1], yc.shape[1]
    tm = _pick(s, (512, 256, 128))
    tn = _pick(d, (1024, 512, 256, 128))
    return pl.pallas_call(
        _outproj_kernel,
        out_shape=jax.ShapeDtypeStruct((s, d), F32),
        grid=(s // tm, d // tn),
        in_specs=[
            pl.BlockSpec((tm, wa), lambda i, j: (i, 0)),
            pl.BlockSpec((tm, wb), lambda i, j: (i, 0)),
            pl.BlockSpec((tm, wc), lambda i, j: (i, 0)),
            pl.BlockSpec((1, wc), lambda i, j: (0, 0)),
            pl.BlockSpec((wa + wb + wc, tn), lambda i, j: (0, j)),
            pl.BlockSpec((tm, tn), lambda i, j: (i, j)),
            pl.BlockSpec((1, tn), lambda i, j: (0, j)),
        ],
        out_specs=pl.BlockSpec((tm, tn), lambda i, j: (i, j)),
        scratch_shapes=[pltpu.VMEM((tm, wc), BF16)],
        compiler_params=_cparams(("parallel", "arbitrary")),
        name="out_projection",
    )(ya, yb, yc, norm_c.reshape(1, wc), w_bf16, x, gate)


def _take_top(work, dst_ref, n):
    for kk in range(n):
        m = jnp.max(work, axis=0, keepdims=True)
        dst_ref[kk:kk + 1, :] = m
        work = jnp.where(work == m, -jnp.inf, work)


def _peer_route_kernel(x_ref, nw_ref, sc_ref, sh_ref, wq_ref, keys_ref,
                       h_out, st_out, tau_out, lse_out, h_scr, top1, top2, cand, best):
    @pl.when(pl.program_id(1) == 0)
    def _():
        y = _rms(x_ref[...], nw_ref[...])
        hf = y * (1.0 + sc_ref[...]) + sh_ref[...]
        h_scr[...] = hf.astype(BF16)
        h_out[...] = jnp.transpose(hf).astype(BF16)

    half = keys_ref.shape[2]
    q16 = _dot(h_scr[...], wq_ref[...]).astype(BF16)
    for part, top in ((0, top1), (1, top2)):
        s_t = _dot_nt(keys_ref[part], q16[:, part * half:(part + 1) * half])
        st_out[part] = s_t
        _take_top(s_t, top, PEER_TOPK)

    k = PEER_TOPK
    cand[0:k, :] = top1[0:1, :] + top2[0:k, :]
    for i in range(1, k // 2):
        cand[k + (i - 1) * (k // 2):k + i * (k // 2), :] = top1[i:i + 1, :] + top2[0:k // 2, :]
    base = k + (k // 2 - 1) * (k // 2)
    cand[base:base + k // 2, :] = top1[k // 2:k, :] + top2[0:1, :]
    _take_top(cand[...], best, k)

    b = best[...]
    m0 = b[0:1, :]
    z = jnp.sum(jnp.exp(b - m0), axis=0, keepdims=True)
    tau_out[...] = b[k - 1:k, :]
    lse_out[...] = m0 + jnp.log(z)


def _peer_route(x, norm_w, sc, sh, wq_bf16, keys_bf16):
    s, d = x.shape
    n_heads, _, n_keys, half = keys_bf16.shape
    assert PEER_TOPK == 16 and n_keys % 8 == 0
    tm = _pick(s, (512, 256, 128))
    n_cand = PEER_TOPK + (PEER_TOPK // 2) * (PEER_TOPK // 2)
    row = lambda i, h: (0, 0)
    return pl.pallas_call(
        _peer_route_kernel,
        out_shape=(jax.ShapeDtypeStruct((d, s), BF16),
                   jax.ShapeDtypeStruct((n_heads, 2, n_keys, s), F32),
                   jax.ShapeDtypeStruct((n_heads, 1, s), F32),
                   jax.ShapeDtypeStruct((n_heads, 1, s), F32)),
        grid=(s // tm, n_heads),
        in_specs=[
            pl.BlockSpec((tm, d), lambda i, h: (i, 0)),
            pl.BlockSpec((1, d), row),
            pl.BlockSpec((1, d), row),
            pl.BlockSpec((1, d), row),
            pl.BlockSpec((d, 2 * half), lambda i, h: (0, h)),
            pl.BlockSpec((None, 2, n_keys, half), lambda i, h: (h, 0, 0, 0)),
        ],
        out_specs=(pl.BlockSpec((d, tm), lambda i, h: (0, i)),
                   pl.BlockSpec((None, 2, n_keys, tm), lambda i, h: (h, 0, 0, i)),
                   pl.BlockSpec((None, 1, tm), lambda i, h: (h, 0, i)),
                   pl.BlockSpec((None, 1, tm), lambda i, h: (h, 0, i))),
        scratch_shapes=[pltpu.VMEM((tm, d), BF16),
                        pltpu.VMEM((PEER_TOPK, tm), F32), pltpu.VMEM((PEER_TOPK, tm), F32),
                        pltpu.VMEM((n_cand, tm), F32), pltpu.VMEM((PEER_TOPK, tm), F32)],
        compiler_params=_cparams(("parallel", "arbitrary")),
        name="peer_route",
    )(x, norm_w, sc, sh, wq_bf16, keys_bf16)


def _peer_dense_kernel(ht_ref, u0_ref, un_ref, vt_ref, st_ref, tau_ref, lse_ref, acc_ref, act_scr):
    j = pl.program_id(1)
    n_heads, _, n_keys, tm = st_ref.shape
    tn = un_ref.shape[0]
    cur = lax.rem(j, 2)

    @pl.when(j == 0)
    def _():
        acc_ref[...] = jnp.zeros_like(acc_ref)
        act_scr[0] = _dot(u0_ref[...], ht_ref[...])

    nxt = _dot(un_ref[...], ht_ref[...])
    act = _gelu_tanh(act_scr[cur])
    blocks = []
    for ab in range(tn // n_keys):
        a = j * (tn // n_keys) + ab
        s1_rows = [st_ref[hd, 0, pl.ds(a, 1), :] for hd in range(n_heads)]
        strips = []
        for ls in range(tm // LANES):
            lanes = slice(ls * LANES, (ls + 1) * LANES)
            w = jnp.zeros((n_keys, LANES), F32)
            for hd in range(n_heads):
                x = st_ref[hd, 1, :, lanes] + s1_rows[hd][:, lanes]
                gate = jnp.exp(x - lse_ref[hd:hd + 1, lanes])
                w = w + jnp.where(x >= tau_ref[hd:hd + 1, lanes], gate, 0.0)
            rows = slice(ab * n_keys, (ab + 1) * n_keys)
            strips.append((w * act[rows, lanes]).astype(BF16))
        blocks.append(jnp.concatenate(strips, axis=1))
    coef = blocks[0] if len(blocks) == 1 else jnp.concatenate(blocks, axis=0)
    acc_ref[...] += _dot(vt_ref[...], coef)
    act_scr[1 - cur] = nxt


def _peer_dense(ht, u_bf16, vt_bf16, st, tau, lse):
    d, s = ht.shape
    n_exp = u_bf16.shape[0]
    n_heads, _, n_keys, _ = st.shape
    tm = _pick(s, (512, 256, 128))
    tn = _pick(n_exp, (512, 256, 128))
    assert tn % n_keys == 0
    last = n_exp // tn - 1
    return pl.pallas_call(
        _peer_dense_kernel,
        out_shape=jax.ShapeDtypeStruct((d, s), F32),
        grid=(s // tm, n_exp // tn),
        in_specs=[
            pl.BlockSpec((d, tm), lambda i, j: (0, i), pipeline_mode=pl.Buffered(1)),
            pl.BlockSpec((tn, d), lambda i, j: (0, 0), pipeline_mode=pl.Buffered(1)),
            pl.BlockSpec((tn, d), lambda i, j: (jnp.minimum(j + 1, last), 0)),
            pl.BlockSpec((d, tn), lambda i, j: (0, j)),
            pl.BlockSpec((n_heads, 2, n_keys, tm), lambda i, j: (0, 0, 0, i),
                         pipeline_mode=pl.Buffered(1)),
            pl.BlockSpec((n_heads, tm), lambda i, j: (0, i)),
            pl.BlockSpec((n_heads, tm), lambda i, j: (0, i)),
        ],
        out_specs=pl.BlockSpec((d, tm), lambda i, j: (0, i)),
        scratch_shapes=[pltpu.VMEM((2, tn, tm), F32)],
        compiler_params=_cparams(("parallel", "arbitrary")),
        name="peer_dense",
    )(ht, u_bf16, u_bf16, vt_bf16, st, tau, lse)


def _peer_residual_kernel(acc_ref, x_ref, g_ref, fw_ref, o_ref, *, final_norm):
    y = x_ref[...] + g_ref[...] * jnp.transpose(acc_ref[...])
    if final_norm:
        y = _rms(y, fw_ref[...])
    o_ref[...] = y


def _peer_residual(acc_t, x, gate, final_w, final_norm):
    s, d = x.shape
    tm = _pick(s, (256, 128))
    row = lambda i: (0, 0)
    return pl.pallas_call(
        functools.partial(_peer_residual_kernel, final_norm=final_norm),
        out_shape=jax.ShapeDtypeStruct((s, d), F32),
        grid=(s // tm,),
        in_specs=[
            pl.BlockSpec((d, tm), lambda i: (0, i)),
            pl.BlockSpec((tm, d), lambda i: (i, 0)),
            pl.BlockSpec((1, d), row),
            pl.BlockSpec((1, d), row),
        ],
        out_specs=pl.BlockSpec((tm, d), lambda i: (i, 0)),
        compiler_params=_cparams(("parallel",)),
        name="peer_residual",
    )(acc_t, x, gate, final_w.reshape(1, d))


def _pack_in_weights(w_in, a_width, b_width, b_heads, q_lora, kv_lora):
    o_beta = 2 * a_width + 4 * b_width
    o_alpha = o_beta + b_heads
    o_q = o_alpha + b_heads
    o_kv = o_q + q_lora
    o_kr = o_kv + kv_lora
    d = w_in.shape[0]
    w_kr = w_in[:, o_kr:o_kr + C_ROPE]
    n_gate_pad = LANES - 2 * b_heads
    parts = [w_in[:, :o_beta], w_in[:, o_q:o_kv], w_in[:, o_kv:o_kr], w_kr, _rope_rot_cols(w_kr),
             w_in[:, o_beta:o_q], jnp.zeros((d, n_gate_pad), w_in.dtype)]
    cols = dict(rec=0, gate=a_width, q=2 * a_width, k=2 * a_width + b_width,
                v=2 * a_width + 2 * b_width, z=2 * a_width + 3 * b_width,
                cq=o_beta, ckv=o_beta + q_lora, ckr=o_beta + q_lora + kv_lora,
                gates=o_beta + q_lora + kv_lora + LANES)
    n = cols["gates"] + LANES
    n_pad = -n % 768 if n >= 768 else 0
    if n_pad:
        parts.append(jnp.zeros((d, n_pad), w_in.dtype))
    return jnp.concatenate(parts, axis=1).astype(BF16), cols


def kernel(x, c, positions, mod_w, mod_layer, norm_mix_w, w_in, lru_conv_w, lru_conv_b, lru_wa, lru_ba, lru_wx, lru_bx, lru_lambda, dn_conv_w, dn_a_log, dn_dt_bias, dn_norm_w, mla_q_norm_w, mla_w_q_up, mla_kv_norm_w, mla_w_kv_up, branch_norm_a, branch_norm_c, w_out, norm_ffn_w, peer_w_query, peer_sub_keys, peer_u, peer_v, final_norm_w):
    bsz, s, d = x.shape
    assert bsz == 1, "kernels are written for a single sequence"
    depth = w_in.shape[0]
    a_width = lru_conv_w.shape[2]
    b_width = dn_conv_w.shape[2] // 3
    b_heads = dn_a_log.shape[1]
    q_lora = mla_q_norm_w.shape[1]
    kv_lora = mla_kv_norm_w.shape[1]
    p_heads, q_dim = peer_w_query.shape[2], peer_w_query.shape[3]

    mod = _modulation(c, mod_w, mod_layer).reshape(depth, N_MOD, 1, d)
    xs = x.reshape(s, d)
    pos = positions.reshape(s)

    for l in range(depth):
        sh_a, sc_a, g_a, sh_f, sc_f, g_f = [mod[l, i] for i in range(N_MOD)]
        w_in_p, cols = _pack_in_weights(w_in[l], a_width, b_width, b_heads, q_lora, kv_lora)
        p = _in_projection(xs, norm_mix_w[l].reshape(1, d), sc_a, sh_a, w_in_p)

        ya = _rglru_group(p, cols["rec"], cols["gate"], lru_conv_w[l], lru_conv_b[l], lru_wa[l],
                          lru_ba[l], lru_wx[l], lru_bx[l], lru_lambda[l], branch_norm_a[l])
        yb = _deltanet_group(p, (cols["q"], cols["k"], cols["v"], cols["z"]), cols["gates"],
                             dn_conv_w[l], dn_a_log[l], dn_dt_bias[l], dn_norm_w[l])
        q, k, v = _mla_projections(p, cols["cq"], cols["ckv"], cols["ckr"], pos, mla_q_norm_w[l],
                                   mla_w_q_up[l], mla_kv_norm_w[l], mla_w_kv_up[l])
        yc = _causal_attention(q, k, v)
        xs = _out_projection(ya, yb, yc, branch_norm_c[l], w_out[l].astype(BF16), xs, g_a)

        wq = peer_w_query[l].reshape(d, p_heads * q_dim).astype(BF16)
        h2, st, tau, lse = _peer_route(xs, norm_ffn_w[l].reshape(1, d), sc_f, sh_f, wq,
                                       peer_sub_keys[l].astype(BF16))
        acc_t = _peer_dense(h2, peer_u[l].astype(BF16), jnp.transpose(peer_v[l]).astype(BF16), st,
                            tau.reshape(p_heads, s), lse.reshape(p_heads, s))
        xs = _peer_residual(acc_t, xs, g_f, final_norm_w, final_norm=(l == depth - 1))
    return xs.reshape(bsz, s, d)
```

```python
import functools
import math

import jax
import jax.numpy as jnp
from jax import lax
from jax.experimental import pallas as pl
from jax.experimental.pallas import tpu as pltpu

F32 = jnp.float32
BF16 = jnp.bfloat16
EPS = 1e-6
N_MOD = 6
LRU_C = 8.0
CONV_W = 4
C_NOPE = 128
C_ROPE = 64
ROPE_THETA = 10000.0
PEER_TOPK = 16
ATTN_TK = 512
ATTN_UNROLL = 4
DN_CHUNK = 128
DN_BASE = 16
LANES = 128
MXU_WIDTH = 256
NEG_BIG = -1e30

_VMEM_LIMIT = 56 * 1024 * 1024


def _cparams(sem, vmem=_VMEM_LIMIT):
    return pltpu.CompilerParams(dimension_semantics=sem, vmem_limit_bytes=vmem)


def _pick(n, candidates):
    for c in candidates:
        if n % c == 0:
            return c
    raise ValueError(f"no tile in {candidates} divides {n}")


def _sigmoid(x):
    return 1.0 / (1.0 + jnp.exp(-x))


def _softplus(x):
    return jnp.maximum(x, 0.0) + jnp.log1p(jnp.exp(-jnp.abs(x)))


def _silu(x):
    return x * _sigmoid(x)


def _gelu_tanh(x):
    c = math.sqrt(2.0 / math.pi)
    return 0.5 * x * (1.0 + jnp.tanh(c * (x + 0.044715 * (x * x * x))))


def _rms(x, w):
    ms = jnp.mean(x * x, axis=-1, keepdims=True)
    return x * lax.rsqrt(ms + EPS) * w


def _dot(a, b):
    return jnp.dot(a, b, preferred_element_type=F32)


def _dot_nt(a, b):
    return lax.dot_general(a, b, (((1,), (1,)), ((), ())), preferred_element_type=F32)


def _mod_kernel(c_ref, w_ref, ml_ref, o_ref, acc_ref):
    k = pl.program_id(1)

    @pl.when(k == 0)
    def _():
        acc_ref[...] = jnp.zeros_like(acc_ref)

    c = c_ref[...]
    sc = jnp.broadcast_to(_silu(c), (8, c.shape[1]))
    acc_ref[...] += _dot(sc, w_ref[...])

    @pl.when(k == pl.num_programs(1) - 1)
    def _():
        o_ref[...] = acc_ref[0:1, :] + ml_ref[...]


def _modulation(c, mod_w, mod_layer):
    d, n = mod_w.shape
    nl = mod_layer.shape[0]
    tk = _pick(d, (1024, 512, 256, 128))
    tn = _pick(n, (2048, 1024, 512, 256, 128))
    ml = mod_layer.reshape(nl, n)
    return pl.pallas_call(
        _mod_kernel,
        out_shape=jax.ShapeDtypeStruct((nl, n), F32),
        grid=(n // tn, d // tk),
        in_specs=[
            pl.BlockSpec((1, tk), lambda j, k: (0, k)),
            pl.BlockSpec((tk, tn), lambda j, k: (k, j)),
            pl.BlockSpec((nl, tn), lambda j, k: (0, j)),
        ],
        out_specs=pl.BlockSpec((nl, tn), lambda j, k: (0, j)),
        scratch_shapes=[pltpu.VMEM((8, tn), F32)],
        compiler_params=_cparams(("parallel", "arbitrary")),
        name="adaln_modulation",
    )(c, mod_w, ml)


def _inproj_kernel(x_ref, nw_ref, sc_ref, sh_ref, w_ref, o_ref, h_ref):
    @pl.when(pl.program_id(1) == 0)
    def _():
        y = _rms(x_ref[...], nw_ref[...])
        h_ref[...] = (y * (1.0 + sc_ref[...]) + sh_ref[...]).astype(BF16)

    o_ref[...] = _dot(h_ref[...], w_ref[...])


def _in_projection(x, norm_w, sc, sh, w_bf16):
    s, d = x.shape
    n = w_bf16.shape[1]
    tm = _pick(s, (512, 256, 128))
    tn = _pick(n, (768, 512, 256, 128))
    row = lambda i, j: (0, 0)
    return pl.pallas_call(
        _inproj_kernel,
        out_shape=jax.ShapeDtypeStruct((s, n), F32),
        grid=(s // tm, n // tn),
        in_specs=[
            pl.BlockSpec((tm, d), lambda i, j: (i, 0)),
            pl.BlockSpec((1, d), row),
            pl.BlockSpec((1, d), row),
            pl.BlockSpec((1, d), row),
            pl.BlockSpec((d, tn), lambda i, j: (0, j)),
        ],
        out_specs=pl.BlockSpec((tm, tn), lambda i, j: (i, j)),
        scratch_shapes=[pltpu.VMEM((tm, d), BF16)],
        compiler_params=_cparams(("parallel", "arbitrary")),
        name="in_projection",
    )(x, norm_w, sc, sh, w_bf16)


def _causal_conv(buf_ref, x, cw_ref):
    t = x.shape[0]
    buf_ref[8:t + 8, :] = x
    y = cw_ref[CONV_W - 1:CONV_W, :] * x
    for j in range(CONV_W - 1):
        off = 8 - (CONV_W - 1) + j
        y = y + cw_ref[j:j + 1, :] * buf_ref[off:off + t, :]
    buf_ref[0:8, :] = buf_ref[t:t + 8, :]
    return y


def _shift_rows(x, d, fill, row_ids):
    return jnp.where(row_ids >= d, pltpu.roll(x, d, axis=0), fill)


def _rglru_kernel(xr_ref, xg_ref, cw_ref, cb_ref, wa_ref, ba_ref, wx_ref, bx_ref,
                  lam_ref, nw_ref, o_ref, xbuf, hprev, *, heads, head_dim):
    t = xr_ref.shape[0]

    @pl.when(pl.program_id(0) == 0)
    def _():
        xbuf[0:8, :] = jnp.zeros((8, xbuf.shape[1]), F32)
        hprev[...] = jnp.zeros_like(hprev)

    u = _causal_conv(xbuf, xr_ref[...], cw_ref) + cb_ref[...]
    ub = u.astype(BF16)
    ra, rx = [], []
    for h in range(heads):
        uh = ub[:, h * head_dim:(h + 1) * head_dim]
        ra.append(_dot(uh, wa_ref[h]))
        rx.append(_dot(uh, wx_ref[h]))
    r = _sigmoid(jnp.concatenate(ra, axis=1) + ba_ref[...])
    gate_i = _sigmoid(jnp.concatenate(rx, axis=1) + bx_ref[...])
    log_a = (-LRU_C) * r * _softplus(-lam_ref[...])
    a = jnp.exp(log_a)
    one_minus_a2 = jnp.tanh(-log_a) * (a * a + 1.0)
    b = jnp.sqrt(jnp.maximum(one_minus_a2, 0.0)) * (gate_i * u)

    row_ids = lax.broadcasted_iota(jnp.int32, a.shape, 0)
    d = 1
    while d < t:
        a_sh = _shift_rows(a, d, 1.0, row_ids)
        b_sh = _shift_rows(b, d, 0.0, row_ids)
        b = a * b_sh + b
        a = a * a_sh
        d *= 2
    hs = b + a * hprev[...]
    hprev[...] = hs[t - 1:t, :]

    y = _gelu_tanh(xg_ref[...]) * hs
    o_ref[...] = _rms(y, nw_ref[...]).astype(BF16)


def _rglru_group(p, col_rec, col_gate, conv_w, conv_b, wa, ba, wx, bx, lam, norm_w):
    s = p.shape[0]
    heads, head_dim = wa.shape[0], wa.shape[1]
    width = heads * head_dim
    t = _pick(s, (256, 128))
    vec = lambda v: v.reshape(1, width)
    full = lambda i: (0, 0)
    return pl.pallas_call(
        functools.partial(_rglru_kernel, heads=heads, head_dim=head_dim),
        out_shape=jax.ShapeDtypeStruct((s, width), BF16),
        grid=(s // t,),
        in_specs=[
            pl.BlockSpec((t, width), lambda i: (i, col_rec // width)),
            pl.BlockSpec((t, width), lambda i: (i, col_gate // width)),
            pl.BlockSpec((CONV_W, width), full),
            pl.BlockSpec((1, width), full),
            pl.BlockSpec((heads, head_dim, head_dim), lambda i: (0, 0, 0)),
            pl.BlockSpec((1, width), full),
            pl.BlockSpec((heads, head_dim, head_dim), lambda i: (0, 0, 0)),
            pl.BlockSpec((1, width), full),
            pl.BlockSpec((1, width), full),
            pl.BlockSpec((1, width), full),
        ],
        out_specs=pl.BlockSpec((t, width), lambda i: (i, 0)),
        scratch_shapes=[pltpu.VMEM((t + 8, width), F32), pltpu.VMEM((1, width), F32)],
        compiler_params=_cparams(("arbitrary",)),
        name="rglru_group",
    )(p, p, conv_w, vec(conv_b), wa.astype(BF16), vec(ba), wx.astype(BF16), vec(bx),
      vec(lam), vec(norm_w))


def _deltanet_kernel(q_ref, k_ref, v_ref, z_ref, g_ref, cwq_ref, cwk_ref, cwv_ref,
                     alog_ref, dtb_ref, nw_ref, o_ref, qbuf, kbuf, vbuf, state,
                     *, n_heads, hp):
    hg = pl.program_id(0)
    t = q_ref.shape[0]
    dh = q_ref.shape[1] // hp
    c = DN_CHUNK
    n_ch = t // c
    assert dh == c, "decay matrix is built from a (chunk, head_dim) lane-replicated tile"

    @pl.when(pl.program_id(1) == 0)
    def _():
        z8 = jnp.zeros((8, hp * dh), F32)
        qbuf[0:8, :] = z8
        kbuf[0:8, :] = z8
        vbuf[0:8, :] = z8
        state[...] = jnp.zeros_like(state)

    xq = _silu(_causal_conv(qbuf, q_ref[...], cwq_ref))
    xk = _silu(_causal_conv(kbuf, k_ref[...], cwk_ref))
    xv = _silu(_causal_conv(vbuf, v_ref[...], cwv_ref))
    gates = g_ref[...]
    lane = lax.broadcasted_iota(jnp.int32, gates.shape, 1)

    rows = lax.broadcasted_iota(jnp.int32, (c, c), 0)
    cols = lax.broadcasted_iota(jnp.int32, (c, c), 1)
    lower = rows >= cols
    strict = rows > cols
    eye = jnp.where(rows == cols, 1.0, 0.0)
    blk = []
    b = DN_BASE
    while b <= c:
        sh = int(math.log2(b))
        blk.append((rows >> sh) == (cols >> sh))
        b *= 2

    pairs = []
    for hh in range(hp):
        h = hg * hp + hh
        hs = slice(hh * dh, (hh + 1) * dh)
        xqh, xkh = xq[:, hs], xk[:, hs]
        qh = xqh * lax.rsqrt(jnp.sum(xqh * xqh, axis=-1, keepdims=True) + EPS) * (dh ** -0.5)
        kh = xkh * lax.rsqrt(jnp.sum(xkh * xkh, axis=-1, keepdims=True) + EPS)
        beta_raw = jnp.sum(jnp.where(lane == h, gates, 0.0), axis=-1, keepdims=True)
        alpha_raw = jnp.sum(jnp.where(lane == h + n_heads, gates, 0.0), axis=-1, keepdims=True)
        beta = jnp.broadcast_to(_sigmoid(beta_raw), (t, dh))
        g = jnp.broadcast_to(alpha_raw, (t, dh))
        g = -jnp.exp(alog_ref[hh]) * _softplus(g + dtb_ref[hh])
        for ci in range(n_ch):
            sl = slice(ci * c, (ci + 1) * c)
            gc = g[sl]
            d = 1
            while d < c:
                gc = gc + _shift_rows(gc, d, 0.0, rows)
                d *= 2
            gct = jnp.transpose(gc)
            decay = jnp.where(lower, jnp.exp(jnp.where(lower, gc - gct, 0.0)), 0.0)
            kc, bc = kh[sl], beta[sl]
            kb = kc * bc
            pairs.append(dict(hh=hh, ci=ci, q=qh[sl], k=kc, kb=kb, k16=kc.astype(BF16),
                              rhs=jnp.concatenate([xv[sl, hs] * bc, kb * jnp.exp(gc)], axis=1),
                              gc=gc, decay=decay, g_last=gc[c - 1:c, :]))

    for pr in pairs:
        pr["lmat"] = jnp.where(strict, _dot_nt(pr["kb"].astype(BF16), pr["k16"]) * pr["decay"], 0.0)
        pr["pk"] = (-jnp.where(blk[0], pr["lmat"], 0.0)).astype(BF16)
        pr["tinv"] = eye + pr["pk"]
    for _ in range(int(math.log2(DN_BASE)) - 1):
        for pr in pairs:
            pr["pk"] = _dot(pr["pk"], pr["pk"]).astype(BF16)
        for pr in pairs:
            pr["tinv"] = pr["tinv"] + _dot(pr["tinv"].astype(BF16), pr["pk"])
    for lvl in range(len(blk) - 1):
        for pr in pairs:
            l_off = jnp.where(blk[lvl + 1], jnp.where(blk[lvl], 0.0, pr["lmat"]), 0.0)
            pr["t16"] = pr["tinv"].astype(BF16)
            pr["tl"] = _dot(pr["t16"], l_off.astype(BF16)).astype(BF16)
        for pr in pairs:
            pr["tinv"] = pr["tinv"] - _dot(pr["tl"], pr["t16"])
    for pr in pairs:
        x = _dot(pr["tinv"].astype(BF16), pr["rhs"].astype(BF16))
        pr["u"], pr["w16"] = x[:, :dh], x[:, dh:].astype(BF16)
        pr["qk16"] = (_dot_nt(pr["q"].astype(BF16), pr["k16"]) * pr["decay"]).astype(BF16)
        pr["q_dec"] = (pr["q"] * jnp.exp(pr["gc"])).astype(BF16)
        k_tail = pr["k"] * jnp.exp(pr["g_last"] - pr["gc"])
        pr["k_tail_t"] = jnp.transpose(k_tail).astype(BF16)
        pr["gl"] = jnp.exp(pr["g_last"])

    sts = [state[hh] for hh in range(hp)]
    outs = [[None] * n_ch for _ in range(hp)]
    for ci in range(n_ch):
        cur = [pr for pr in pairs if pr["ci"] == ci]
        st16 = [sts[pr["hh"]].astype(BF16) for pr in cur]
        v16 = [(pr["u"] - _dot(pr["w16"], s16)).astype(BF16) for pr, s16 in zip(cur, st16)]
        for pr, s16, vn in zip(cur, st16, v16):
            outs[pr["hh"]][ci] = _dot(pr["q_dec"], s16) + _dot(pr["qk16"], vn)
        for pr, vn in zip(cur, v16):
            sts[pr["hh"]] = sts[pr["hh"]] * pr["gl"] + _dot(pr["k_tail_t"], vn)
    for hh in range(hp):
        state[hh] = sts[hh]
        hs = slice(hh * dh, (hh + 1) * dh)
        o = outs[hh][0] if n_ch == 1 else jnp.concatenate(outs[hh], axis=0)
        o = _rms(o, nw_ref[...]) * _silu(z_ref[:, hs])
        o_ref[:, hs] = o.astype(BF16)


def _deltanet_group(p, cols, col_gates, conv_w, a_log, dt_bias, norm_w):
    s = p.shape[0]
    n_heads = a_log.shape[0]
    dh = norm_w.shape[0]
    width = n_heads * dh
    col_q, col_k, col_v, col_z = cols
    hp = max(h_ for h_ in (4, 2, 1)
             if n_heads % h_ == 0 and all(col % (h_ * dh) == 0 for col in cols))
    t = _pick(s, (256, 128))
    lanes = lambda v: jnp.broadcast_to(v.reshape(n_heads, 1, 1), (n_heads, 1, dh))

    def pcol(col):
        return pl.BlockSpec((t, hp * dh), lambda h, i: (i, col // (hp * dh) + h))

    def ccol(grp):
        return pl.BlockSpec((CONV_W, hp * dh), lambda h, i: (0, grp * (n_heads // hp) + h))

    return pl.pallas_call(
        functools.partial(_deltanet_kernel, n_heads=n_heads, hp=hp),
        out_shape=jax.ShapeDtypeStruct((s, width), BF16),
        grid=(n_heads // hp, s // t),
        in_specs=[
            pcol(col_q), pcol(col_k), pcol(col_v), pcol(col_z),
            pl.BlockSpec((t, LANES), lambda h, i: (i, col_gates // LANES)),
            ccol(0), ccol(1), ccol(2),
            pl.BlockSpec((hp, 1, dh), lambda h, i: (h, 0, 0)),
            pl.BlockSpec((hp, 1, dh), lambda h, i: (h, 0, 0)),
            pl.BlockSpec((1, dh), lambda h, i: (0, 0)),
        ],
        out_specs=pl.BlockSpec((t, hp * dh), lambda h, i: (i, h)),
        scratch_shapes=[pltpu.VMEM((t + 8, hp * dh), F32)] * 3 + [pltpu.VMEM((hp, dh, dh), F32)],
        compiler_params=_cparams(("parallel", "arbitrary")),
        name="deltanet_group",
    )(p, p, p, p, p, conv_w, conv_w, conv_w, lanes(a_log), lanes(dt_bias),
      norm_w.reshape(1, dh))


def _mla_proj_kernel(cq_ref, ckv_ref, ckr_ref, pos_ref, inv_ref, qn_ref, kvn_ref,
                     wq_ref, wkv_ref, q_out, k_out, v_out, hq, hkv, cos_t, sin_t, kpe,
                     *, scale):
    @pl.when(pl.program_id(1) == 0)
    def _():
        hq[...] = _rms(cq_ref[...], qn_ref[...]).astype(BF16)
        hkv[...] = _rms(ckv_ref[...], kvn_ref[...]).astype(BF16)
        ang = pos_ref[...].astype(F32) * inv_ref[...]
        lane = lax.broadcasted_iota(jnp.int32, ang.shape, 1)
        cs = jnp.where(lane < C_ROPE, jnp.cos(ang), 0.0)
        sn = jnp.where(lane < C_ROPE, jnp.sin(ang), 0.0)
        cos_t[...] = cs
        sin_t[...] = sn
        r2 = ckr_ref[...]
        kpe[...] = (r2 * cs + pltpu.roll(r2, C_ROPE, axis=1) * sn).astype(BF16)

    rq = _dot(hq[...], wq_ref[...])
    r2 = rq[:, C_NOPE:]
    q_pe = r2 * cos_t[...] + pltpu.roll(r2, C_ROPE, axis=1) * sin_t[...]
    q_out[:, :C_NOPE] = (rq[:, :C_NOPE] * scale).astype(BF16)
    q_out[:, C_NOPE:] = (q_pe * scale).astype(BF16)

    rkv = _dot(hkv[...], wkv_ref[...])
    k_out[:, :C_NOPE] = rkv[:, :C_NOPE].astype(BF16)
    k_out[:, C_NOPE:] = kpe[...]
    tkv = v_out.shape[2]
    for ck in range(v_out.shape[0]):
        v_out[ck] = jnp.transpose(rkv[ck * tkv:(ck + 1) * tkv, C_NOPE:]).astype(BF16)


def _rope_rot_cols(w):
    half = C_ROPE // 2
    return jnp.concatenate([-w[..., half:], w[..., :half]], axis=-1)


def _mla_projections(p, col_q, col_kv, col_kr, positions, q_norm_w, w_q_up, kv_norm_w, w_kv_up):
    s = p.shape[0]
    q_lora, n_heads, dqk = w_q_up.shape
    kv_lora = w_kv_up.shape[0]
    dv = w_kv_up.shape[2] - C_NOPE
    assert dqk == C_NOPE + C_ROPE and 2 * C_ROPE == LANES
    t = _pick(s, (512, 256, 128))
    tkv = min(t, ATTN_TK)
    w_pe = w_q_up[..., C_NOPE:]
    wq = jnp.concatenate([w_q_up[..., :C_NOPE], w_pe, _rope_rot_cols(w_pe)], axis=-1)
    wq = jnp.transpose(wq, (1, 0, 2)).astype(BF16)
    wkv = jnp.transpose(w_kv_up, (1, 0, 2)).astype(BF16)
    inv = ROPE_THETA ** (-jnp.arange(0, C_ROPE, 2, dtype=F32) / C_ROPE)
    inv = jnp.concatenate([inv, inv, jnp.zeros((LANES - C_ROPE,), F32)]).reshape(1, LANES)
    dk = C_NOPE + LANES
    scale = float(dqk) ** -0.5 * math.log2(math.e)
    full = lambda i, h: (0, 0)
    return pl.pallas_call(
        functools.partial(_mla_proj_kernel, scale=scale),
        out_shape=(jax.ShapeDtypeStruct((n_heads, s, dk), BF16),
                   jax.ShapeDtypeStruct((n_heads, s, dk), BF16),
                   jax.ShapeDtypeStruct((n_heads, s // tkv, dv, tkv), BF16)),
        grid=(s // t, n_heads),
        in_specs=[
            pl.BlockSpec((t, q_lora), lambda i, h: (i, col_q // q_lora)),
            pl.BlockSpec((t, kv_lora), lambda i, h: (i, col_kv // kv_lora)),
            pl.BlockSpec((t, LANES), lambda i, h: (i, col_kr // LANES)),
            pl.BlockSpec((t, 1), lambda i, h: (i, 0)),
            pl.BlockSpec((1, LANES), full),
            pl.BlockSpec((1, q_lora), full),
            pl.BlockSpec((1, kv_lora), full),
            pl.BlockSpec((None, q_lora, dk), lambda i, h: (h, 0, 0)),
            pl.BlockSpec((None, kv_lora, C_NOPE + dv), lambda i, h: (h, 0, 0)),
        ],
        out_specs=(pl.BlockSpec((None, t, dk), lambda i, h: (h, i, 0)),
                   pl.BlockSpec((None, t, dk), lambda i, h: (h, i, 0)),
                   pl.BlockSpec((None, t // tkv, dv, tkv), lambda i, h: (h, i, 0, 0))),
        scratch_shapes=[pltpu.VMEM((t, q_lora), BF16), pltpu.VMEM((t, kv_lora), BF16),
                        pltpu.VMEM((t, LANES), F32), pltpu.VMEM((t, LANES), F32),
                        pltpu.VMEM((t, LANES), BF16)],
        compiler_params=_cparams(("parallel", "arbitrary")),
        name="mla_projections",
    )(p, p, p, positions.reshape(s, 1), inv, q_norm_w.reshape(1, q_lora),
      kv_norm_w.reshape(1, kv_lora), wq, wkv)


def _flash_kernel(q_ref, k_ref, vt_ref, o_ref, s_scr):
    i = pl.program_id(1)
    tq = q_ref.shape[0]
    tk, dv = k_ref.shape[1], vt_ref.shape[1]
    q = q_ref[...]

    def scores(j):
        return _dot_nt(k_ref[j], q)

    def update(state, s, j):
        m, l, acc = state
        m_new = jnp.maximum(m, jnp.max(s, axis=0, keepdims=True))
        alpha = jnp.exp2(m - m_new)
        p = jnp.exp2(s - m_new)
        l = alpha * l + jnp.sum(p, axis=0, keepdims=True)
        acc = alpha * acc + _dot(vt_ref[j], p.astype(BF16))
        return m_new, l, acc

    def body(j, state):
        s_next = scores(j + 1)
        state = update(state, s_scr[...], j)
        s_scr[...] = s_next
        return state

    def body_unrolled(g, state):
        for r in range(ATTN_UNROLL):
            state = body(g * ATTN_UNROLL + r, state)
        return state

    s_scr[...] = scores(0)
    init = (jnp.full((1, tq), NEG_BIG, F32), jnp.zeros((1, tq), F32), jnp.zeros((dv, tq), F32))
    n_main = lax.shift_right_logical(i, int(math.log2(ATTN_UNROLL)))
    state = lax.fori_loop(0, n_main, body_unrolled, init)
    state = lax.fori_loop(n_main * ATTN_UNROLL, i, body, state)
    rows = lax.broadcasted_iota(jnp.int32, (tk, tq), 0)
    cols = lax.broadcasted_iota(jnp.int32, (tk, tq), 1)
    _, l, acc = update(state, jnp.where(rows <= cols, s_scr[...], NEG_BIG), i)
    o_ref[...] = jnp.transpose(acc / l).astype(o_ref.dtype)


def _causal_attention(q, k, vt):
    n_heads, s, dk = q.shape
    _, n_chunks, dv, tk = vt.shape
    k = k.reshape(n_heads, n_chunks, tk, dk)
    return pl.pallas_call(
        _flash_kernel,
        out_shape=jax.ShapeDtypeStruct((s, n_heads * dv), BF16),
        grid=(n_heads, n_chunks),
        in_specs=[
            pl.BlockSpec((None, tk, dk), lambda h, i: (h, i, 0)),
            pl.BlockSpec((None, n_chunks, tk, dk), lambda h, i: (h, 0, 0, 0)),
            pl.BlockSpec((None, n_chunks, dv, tk), lambda h, i: (h, 0, 0, 0)),
        ],
        out_specs=pl.BlockSpec((tk, dv), lambda h, i: (i, h)),
        scratch_shapes=[pltpu.VMEM((tk, tk), F32)],
        compiler_params=_cparams(("parallel", "arbitrary")),
        name="mla_flash_attention",
    )(q, k, vt)


def _outproj_kernel(ya_ref, yb_ref, yc_ref, nc_ref, w_ref, x_ref, g_ref, o_ref, ycn):
    wa, wb = ya_ref.shape[1], yb_ref.shape[1]

    @pl.when(pl.program_id(1) == 0)
    def _():
        ycn[...] = _rms(yc_ref[...].astype(F32), nc_ref[...]).astype(BF16)

    acc = _dot(ya_ref[...], w_ref[0:wa, :])
    acc += _dot(yb_ref[...], w_ref[wa:wa + wb, :])
    acc += _dot(ycn[...], w_ref[wa + wb:, :])
    o_ref[...] = x_ref[...] + g_ref[...] * acc


def _out_projection(ya, yb, yc, norm_c, w_bf16, x, gate):
    s, d = x.shape
    wa, wb, wc = ya.shape[1], yb.shape[1], yc.shape[1]
    tm = _pick(s, (512, 256, 128))
    tn = _pick(d, (1024, 512, 256, 128))
    return pl.pallas_call(
        _outproj_kernel,
        out_shape=jax.ShapeDtypeStruct((s, d), F32),
        grid=(s // tm, d // tn),
        in_specs=[
            pl.BlockSpec((tm, wa), lambda i, j: (i, 0)),
            pl.BlockSpec((tm, wb), lambda i, j: (i, 0)),
            pl.BlockSpec((tm, wc), lambda i, j: (i, 0)),
            pl.BlockSpec((1, wc), lambda i, j: (0, 0)),
            pl.BlockSpec((wa + wb + wc, tn), lambda i, j: (0, j)),
            pl.BlockSpec((tm, tn), lambda i, j: (i, j)),
            pl.BlockSpec((1, tn), lambda i, j: (0, j)),
        ],
        out_specs=pl.BlockSpec((tm, tn), lambda i, j: (i, j)),
        scratch_shapes=[pltpu.VMEM((tm, wc), BF16)],
        compiler_params=_cparams(("parallel", "arbitrary")),
        name="out_projection",
    )(ya, yb, yc, norm_c.reshape(1, wc), w_bf16, x, gate)


def _take_top(works, dsts, n):
    works = list(works)
    for kk in range(n):
        for ch, dst in enumerate(dsts):
            m = jnp.max(works[ch], axis=0, keepdims=True)
            dst[kk:kk + 1, :] = m
            works[ch] = jnp.where(works[ch] == m, -jnp.inf, works[ch])


def _peer_route_kernel(x_ref, nw_ref, sc_ref, sh_ref, wq_ref, keys_ref,
                       h_out, st_out, tau_out, lse_out, h_scr, top, cand, best):
    @pl.when(pl.program_id(1) == 0)
    def _():
        y = _rms(x_ref[...], nw_ref[...])
        hf = y * (1.0 + sc_ref[...]) + sh_ref[...]
        h_scr[...] = hf.astype(BF16)
        h_out[...] = jnp.transpose(hf).astype(BF16)

    hp, _, _, half = keys_ref.shape
    q16 = _dot(h_scr[...], wq_ref[...]).astype(BF16)
    works, dsts = [], []
    for hh in range(hp):
        for part in range(2):
            col = (hh * 2 + part) * half
            s_t = _dot_nt(keys_ref[hh, part], q16[:, col:col + half])
            st_out[hh, part] = s_t
            works.append(s_t)
            dsts.append(top.at[hh, part])
    _take_top(works, dsts, PEER_TOPK)

    k = PEER_TOPK
    base = k + (k // 2 - 1) * (k // 2)
    for hh in range(hp):
        top1, top2, cd = top.at[hh, 0], top.at[hh, 1], cand.at[hh]
        cd[0:k, :] = top1[0:1, :] + top2[0:k, :]
        for i in range(1, k // 2):
            cd[k + (i - 1) * (k // 2):k + i * (k // 2), :] = top1[i:i + 1, :] + top2[0:k // 2, :]
        cd[base:base + k // 2, :] = top1[k // 2:k, :] + top2[0:1, :]
    _take_top([cand[hh] for hh in range(hp)], [best.at[hh] for hh in range(hp)], k)

    for hh in range(hp):
        b = best[hh]
        m0 = b[0:1, :]
        z = jnp.sum(jnp.exp(b - m0), axis=0, keepdims=True)
        tau_out[hh] = b[k - 1:k, :]
        lse_out[hh] = m0 + jnp.log(z)


def _peer_route(x, norm_w, sc, sh, wq_bf16, keys_bf16):
    s, d = x.shape
    n_heads, _, n_keys, half = keys_bf16.shape
    assert PEER_TOPK == 16 and n_keys % 8 == 0
    tm = _pick(s, (512, 256, 128))
    hp = 2 if n_heads % 2 == 0 else 1
    n_cand = PEER_TOPK + (PEER_TOPK // 2) * (PEER_TOPK // 2)
    row = lambda i, h: (0, 0)
    return pl.pallas_call(
        _peer_route_kernel,
        out_shape=(jax.ShapeDtypeStruct((d, s), BF16),
                   jax.ShapeDtypeStruct((n_heads, 2, n_keys, s), F32),
                   jax.ShapeDtypeStruct((n_heads, 1, s), F32),
                   jax.ShapeDtypeStruct((n_heads, 1, s), F32)),
        grid=(s // tm, n_heads // hp),
        in_specs=[
            pl.BlockSpec((tm, d), lambda i, h: (i, 0)),
            pl.BlockSpec((1, d), row),
            pl.BlockSpec((1, d), row),
            pl.BlockSpec((1, d), row),
            pl.BlockSpec((d, hp * 2 * half), lambda i, h: (0, h)),
            pl.BlockSpec((hp, 2, n_keys, half), lambda i, h: (h, 0, 0, 0)),
        ],
        out_specs=(pl.BlockSpec((d, tm), lambda i, h: (0, i)),
                   pl.BlockSpec((hp, 2, n_keys, tm), lambda i, h: (h, 0, 0, i)),
                   pl.BlockSpec((hp, 1, tm), lambda i, h: (h, 0, i)),
                   pl.BlockSpec((hp, 1, tm), lambda i, h: (h, 0, i))),
        scratch_shapes=[pltpu.VMEM((tm, d), BF16),
                        pltpu.VMEM((hp, 2, PEER_TOPK, tm), F32),
                        pltpu.VMEM((hp, n_cand, tm), F32), pltpu.VMEM((hp, PEER_TOPK, tm), F32)],
        compiler_params=_cparams(("parallel", "arbitrary")),
        name="peer_route",
    )(x, norm_w, sc, sh, wq_bf16, keys_bf16)


def _peer_dense_kernel(ht_ref, u0_ref, un_ref, vt_ref, st_ref, tau_ref, lse_ref, acc_ref, act_scr):
    j = pl.program_id(1)
    n_heads, _, n_keys, tm = st_ref.shape
    tn = un_ref.shape[0]
    cur = lax.rem(j, 2)

    @pl.when(j == 0)
    def _():
        acc_ref[...] = jnp.zeros_like(acc_ref)
        act_scr[0] = _dot(u0_ref[...], ht_ref[...])

    nxt = _dot(un_ref[...], ht_ref[...])
    act = _gelu_tanh(act_scr[cur])
    blocks = []
    for ab in range(tn // n_keys):
        a = j * (tn // n_keys) + ab
        s1_rows = [st_ref[hd, 0, pl.ds(a, 1), :] for hd in range(n_heads)]
        strips = []
        for ls in range(tm // LANES):
            lanes = slice(ls * LANES, (ls + 1) * LANES)
            w = jnp.zeros((n_keys, LANES), F32)
            for hd in range(n_heads):
                x = st_ref[hd, 1, :, lanes] + s1_rows[hd][:, lanes]
                gate = jnp.exp(x - lse_ref[hd:hd + 1, lanes])
                w = w + jnp.where(x >= tau_ref[hd:hd + 1, lanes], gate, 0.0)
            rows = slice(ab * n_keys, (ab + 1) * n_keys)
            strips.append((w * act[rows, lanes]).astype(BF16))
        blocks.append(jnp.concatenate(strips, axis=1))
    coef = blocks[0] if len(blocks) == 1 else jnp.concatenate(blocks, axis=0)
    acc_ref[...] += _dot(vt_ref[...], coef)
    act_scr[1 - cur] = nxt


def _peer_dense(ht, u_bf16, vt_bf16, st, tau, lse):
    d, s = ht.shape
    n_exp = u_bf16.shape[0]
    n_heads, _, n_keys, _ = st.shape
    tm = _pick(s, (512, 256, 128))
    tn = _pick(n_exp, (512, 256, 128))
    assert tn % n_keys == 0
    last = n_exp // tn - 1
    return pl.pallas_call(
        _peer_dense_kernel,
        out_shape=jax.ShapeDtypeStruct((d, s), F32),
        grid=(s // tm, n_exp // tn),
        in_specs=[
            pl.BlockSpec((d, tm), lambda i, j: (0, i), pipeline_mode=pl.Buffered(1)),
            pl.BlockSpec((tn, d), lambda i, j: (0, 0), pipeline_mode=pl.Buffered(1)),
            pl.BlockSpec((tn, d), lambda i, j: (jnp.minimum(j + 1, last), 0)),
            pl.BlockSpec((d, tn), lambda i, j: (0, j)),
            pl.BlockSpec((n_heads, 2, n_keys, tm), lambda i, j: (0, 0, 0, i),
                         pipeline_mode=pl.Buffered(1)),
            pl.BlockSpec((n_heads, tm), lambda i, j: (0, i)),
            pl.BlockSpec((n_heads, tm), lambda i, j: (0, i)),
        ],
        out_specs=pl.BlockSpec((d, tm), lambda i, j: (0, i)),
        scratch_shapes=[pltpu.VMEM((2, tn, tm), F32)],
        compiler_params=_cparams(("parallel", "arbitrary")),
        name="peer_dense",
    )(ht, u_bf16, u_bf16, vt_bf16, st, tau, lse)


def _peer_residual_kernel(acc_ref, x_ref, g_ref, fw_ref, o_ref, *, final_norm):
    y = x_ref[...] + g_ref[...] * jnp.transpose(acc_ref[...])
    if final_norm:
        y = _rms(y, fw_ref[...])
    o_ref[...] = y


def _peer_residual(acc_t, x, gate, final_w, final_norm):
    s, d = x.shape
    tm = _pick(s, (256, 128))
    row = lambda i: (0, 0)
    return pl.pallas_call(
        functools.partial(_peer_residual_kernel, final_norm=final_norm),
        out_shape=jax.ShapeDtypeStruct((s, d), F32),
        grid=(s // tm,),
        in_specs=[
            pl.BlockSpec((d, tm), lambda i: (0, i)),
            pl.BlockSpec((tm, d), lambda i: (i, 0)),
            pl.BlockSpec((1, d), row),
            pl.BlockSpec((1, d), row),
        ],
        out_specs=pl.BlockSpec((tm, d), lambda i: (i, 0)),
        compiler_params=_cparams(("parallel",)),
        name="peer_residual",
    )(acc_t, x, gate, final_w.reshape(1, d))


def _pack_in_weights(w_in, a_width, b_width, b_heads, q_lora, kv_lora):
    o_beta = 2 * a_width + 4 * b_width
    o_alpha = o_beta + b_heads
    o_q = o_alpha + b_heads
    o_kv = o_q + q_lora
    o_kr = o_kv + kv_lora
    d = w_in.shape[0]
    w_kr = w_in[:, o_kr:o_kr + C_ROPE]
    n_gate_pad = LANES - 2 * b_heads
    parts = [w_in[:, :o_beta], w_in[:, o_q:o_kv], w_in[:, o_kv:o_kr], w_kr, _rope_rot_cols(w_kr),
             w_in[:, o_beta:o_q], jnp.zeros((d, n_gate_pad), w_in.dtype)]
    cols = dict(rec=0, gate=a_width, q=2 * a_width, k=2 * a_width + b_width,
                v=2 * a_width + 2 * b_width, z=2 * a_width + 3 * b_width,
                cq=o_beta, ckv=o_beta + q_lora, ckr=o_beta + q_lora + kv_lora,
                gates=o_beta + q_lora + kv_lora + LANES)
    n = cols["gates"] + LANES
    n_pad = -n % 768 if n >= 768 else 0
    if n_pad:
        parts.append(jnp.zeros((d, n_pad), w_in.dtype))
    return jnp.concatenate(parts, axis=1).astype(BF16), cols


def kernel(x, c, positions, mod_w, mod_layer, norm_mix_w, w_in, lru_conv_w, lru_conv_b, lru_wa, lru_ba, lru_wx, lru_bx, lru_lambda, dn_conv_w, dn_a_log, dn_dt_bias, dn_norm_w, mla_q_norm_w, mla_w_q_up, mla_kv_norm_w, mla_w_kv_up, branch_norm_a, branch_norm_c, w_out, norm_ffn_w, peer_w_query, peer_sub_keys, peer_u, peer_v, final_norm_w):
    bsz, s, d = x.shape
    assert bsz == 1, "kernels are written for a single sequence"
    depth = w_in.shape[0]
    a_width = lru_conv_w.shape[2]
    b_width = dn_conv_w.shape[2] // 3
    b_heads = dn_a_log.shape[1]
    q_lora = mla_q_norm_w.shape[1]
    kv_lora = mla_kv_norm_w.shape[1]
    p_heads, q_dim = peer_w_query.shape[2], peer_w_query.shape[3]

    mod = _modulation(c, mod_w, mod_layer).reshape(depth, N_MOD, 1, d)
    xs = x.reshape(s, d)
    pos = positions.reshape(s)

    for l in range(depth):
        sh_a, sc_a, g_a, sh_f, sc_f, g_f = [mod[l, i] for i in range(N_MOD)]
        w_in_p, cols = _pack_in_weights(w_in[l], a_width, b_width, b_heads, q_lora, kv_lora)
        p = _in_projection(xs, norm_mix_w[l].reshape(1, d), sc_a, sh_a, w_in_p)

        ya = _rglru_group(p, cols["rec"], cols["gate"], lru_conv_w[l], lru_conv_b[l], lru_wa[l],
                          lru_ba[l], lru_wx[l], lru_bx[l], lru_lambda[l], branch_norm_a[l])
        yb = _deltanet_group(p, (cols["q"], cols["k"], cols["v"], cols["z"]), cols["gates"],
                             dn_conv_w[l], dn_a_log[l], dn_dt_bias[l], dn_norm_w[l])
        q, k, v = _mla_projections(p, cols["cq"], cols["ckv"], cols["ckr"], pos, mla_q_norm_w[l],
                                   mla_w_q_up[l], mla_kv_norm_w[l], mla_w_kv_up[l])
        yc = _causal_attention(q, k, v)
        xs = _out_projection(ya, yb, yc, branch_norm_c[l], w_out[l].astype(BF16), xs, g_a)

        wq = peer_w_query[l].reshape(d, p_heads * q_dim).astype(BF16)
        h2, st, tau, lse = _peer_route(xs, norm_ffn_w[l].reshape(1, d), sc_f, sh_f, wq,
                                       peer_sub_keys[l].astype(BF16))
        acc_t = _peer_dense(h2, peer_u[l].astype(BF16), jnp.transpose(peer_v[l]).astype(BF16), st,
                            tau.reshape(p_heads, s), lse.reshape(p_heads, s))
        xs = _peer_residual(acc_t, xs, g_f, final_norm_w, final_norm=(l == depth - 1))
    return xs.reshape(bsz, s, d)
```
